```python
import math
import jax, jax.numpy as jnp
from jax import lax
import numpy as np

D_MODEL = 2048
BATCH = 2
SEQ = 8192
DEPTH = 4

HEAD_DIM = 64
D_MIX = D_MODEL
SWA_HEADS = D_MIX // (2 * HEAD_DIM)
SWA_KV_HEADS = SWA_HEADS // 4
SB_HEADS = D_MIX // (2 * HEAD_DIM)
GQA_GROUP = SWA_HEADS // SWA_KV_HEADS
WINDOW = 128
BLOCK = 128
N_BUCKETS = 32
MAX_DISTANCE = 128
D_FF = 4 * D_MODEL
EPS = 1e-6

SWA_Q_W = SWA_HEADS * HEAD_DIM
SWA_KV_W = SWA_KV_HEADS * HEAD_DIM
SB_W = SB_HEADS * HEAD_DIM
D_IN = SWA_Q_W + 2 * SWA_KV_W + 3 * SB_W
D_CAT = SWA_Q_W + SB_W

kernel_name = "hymba_swa_sink_stickbreaking_sqrelu"


def rmsnorm(x, g):
    xf = x.astype(jnp.float32)
    y = xf * lax.rsqrt(jnp.mean(xf * xf, axis=-1, keepdims=True) + EPS) * g.astype(jnp.float32)
    return y.astype(x.dtype)


def t5_causal_bucket_np(dist):
    max_exact = N_BUCKETS // 2
    d = np.maximum(dist, 0)
    ratio = np.maximum(d, 1).astype(np.float32) / max_exact
    large = max_exact + (np.log(ratio) / math.log(MAX_DISTANCE / max_exact)
                         * (N_BUCKETS - max_exact)).astype(np.int32)
    large = np.minimum(large, N_BUCKETS - 1)
    return np.where(d < max_exact, d, large).astype(np.int32)


def swa_attention(q, k, v, sinks, rel_bias):
    B, S = q.shape[0], q.shape[1]
    nb = S // BLOCK
    qb = q.reshape(B, nb, BLOCK, SWA_KV_HEADS, GQA_GROUP, HEAD_DIM)
    kb = k.reshape(B, nb, BLOCK, SWA_KV_HEADS, HEAD_DIM)
    vb = v.reshape(B, nb, BLOCK, SWA_KV_HEADS, HEAD_DIM)
    pad = jnp.zeros_like(kb[:, :1])
    k_prev = jnp.concatenate([pad, kb[:, :-1]], axis=1)
    v_prev = jnp.concatenate([pad, vb[:, :-1]], axis=1)
    k_ext = jnp.concatenate([k_prev, kb], axis=2)
    v_ext = jnp.concatenate([v_prev, vb], axis=2)

    qi = np.arange(BLOCK)[:, None]
    kj = np.arange(2 * BLOCK)[None, :]
    dist = qi + BLOCK - kj
    in_window = (dist >= 0) & (dist < WINDOW)
    not_pad = (np.arange(nb)[:, None, None] > 0) | (kj >= BLOCK)[None]
    valid = jnp.asarray(in_window[None] & not_pad).reshape(1, nb, 1, 1, BLOCK, 2 * BLOCK)

    bucket = jnp.asarray(t5_causal_bucket_np(dist))
    bias = jnp.take(rel_bias.astype(jnp.float32), bucket, axis=0)
    bias = bias.reshape(BLOCK, 2 * BLOCK, SWA_KV_HEADS, GQA_GROUP).transpose(2, 3, 0, 1)
    bias = bias.reshape(1, 1, SWA_KV_HEADS, GQA_GROUP, BLOCK, 2 * BLOCK)

    scores = jnp.einsum('bnqhgd,bnshd->bnhgqs', qb, k_ext).astype(jnp.float32)
    scores = scores * (1.0 / math.sqrt(HEAD_DIM)) + bias
    scores = jnp.where(valid, scores, -jnp.inf)
    sink = jnp.broadcast_to(sinks.astype(jnp.float32).reshape(1, 1, SWA_KV_HEADS, GQA_GROUP, 1, 1),
                            (B, nb, SWA_KV_HEADS, GQA_GROUP, BLOCK, 1))
    probs = jax.nn.softmax(jnp.concatenate([scores, sink], axis=-1), axis=-1)[..., :2 * BLOCK]
    out = jnp.einsum('bnhgqs,bnshd->bnqhgd', probs.astype(v.dtype), v_ext)
    return out.reshape(B, S, SWA_HEADS * HEAD_DIM)


def stick_breaking_attention(q, k, v):
    B, S = q.shape[0], q.shape[1]
    nb = S // BLOCK
    scale = 1.0 / math.sqrt(HEAD_DIM)
    qb = q.reshape(B, nb, BLOCK, SB_HEADS, HEAD_DIM).transpose(1, 0, 3, 2, 4)
    kt = k.transpose(0, 2, 1, 3)
    vt = v.transpose(0, 2, 1, 3)
    pos_k = jnp.arange(S, dtype=jnp.int32)
    blk_ids = jnp.arange(nb, dtype=jnp.int32)

    def block(args):
        q_blk, i = args
        pos_q = i * BLOCK + jnp.arange(BLOCK, dtype=jnp.int32)
        causal = (pos_k[None, :] < pos_q[:, None])[None, None]
        z = jnp.einsum('bhqd,bhsd->bhqs', q_blk, kt).astype(jnp.float32) * scale
        log_beta = jax.nn.log_sigmoid(z)
        log_1m = jnp.where(causal, jax.nn.log_sigmoid(-z), 0.0)
        later = lax.cumsum(log_1m, axis=3, reverse=True) - log_1m
        w = jnp.where(causal, jnp.exp(log_beta + later), 0.0)
        return jnp.einsum('bhqs,bhsd->bhqd', w.astype(v.dtype), vt)

    out = lax.map(block, (qb, blk_ids))
    return out.transpose(1, 0, 3, 2, 4).reshape(B, S, SB_HEADS * HEAD_DIM)


def setup_inputs(seed: int = 0) -> dict:
    key = jax.random.key(seed)
    ks = jax.random.split(key, 13)
    f32 = jnp.float32
    x = jax.random.normal(ks[0], (BATCH, SEQ, D_MODEL), f32)
    norm_attn_g = 1.0 + 0.02 * jax.random.normal(ks[1], (DEPTH, D_MODEL), f32)
    w_in = jax.random.normal(ks[2], (DEPTH, D_MODEL, D_IN), f32) * D_MODEL ** -0.5
    q_norm_g = 1.0 + 0.02 * jax.random.normal(ks[3], (DEPTH, HEAD_DIM), f32)
    k_norm_g = 1.0 + 0.02 * jax.random.normal(ks[4], (DEPTH, HEAD_DIM), f32)
    sinks = 0.5 * jax.random.normal(ks[5], (DEPTH, SWA_HEADS), f32)
    rel_bias = 0.5 * jax.random.normal(ks[6], (N_BUCKETS, SWA_HEADS), f32)
    swa_out_g = 1.0 + 0.02 * jax.random.normal(ks[7], (DEPTH, SWA_Q_W), f32)
    sb_out_g = 1.0 + 0.02 * jax.random.normal(ks[8], (DEPTH, SB_W), f32)
    w_out = jax.random.normal(ks[9], (DEPTH, D_CAT, D_MODEL), f32) * D_CAT ** -0.5
    norm_mlp_g = 1.0 + 0.02 * jax.random.normal(ks[10], (DEPTH, D_MODEL), f32)
    w_up = jax.random.normal(ks[11], (DEPTH, D_MODEL, D_FF), f32) * D_MODEL ** -0.5
    w_down = jax.random.normal(ks[12], (DEPTH, D_FF, D_MODEL), f32) * D_FF ** -0.5
    return {"x": x, "norm_attn_g": norm_attn_g, "w_in": w_in, "q_norm_g": q_norm_g,
            "k_norm_g": k_norm_g, "sinks": sinks, "rel_bias": rel_bias,
            "swa_out_g": swa_out_g, "sb_out_g": sb_out_g, "w_out": w_out,
            "norm_mlp_g": norm_mlp_g, "w_up": w_up, "w_down": w_down}


def reference(x, norm_attn_g, w_in, q_norm_g, k_norm_g, sinks, rel_bias,
              swa_out_g, sb_out_g, w_out, norm_mlp_g, w_up, w_down):
    B, S = x.shape[0], x.shape[1]
    o1 = SWA_Q_W
    o2 = o1 + SWA_KV_W
    o3 = o2 + SWA_KV_W
    o4 = o3 + SB_W
    o5 = o4 + SB_W
    for l in range(DEPTH):
        h = rmsnorm(x, norm_attn_g[l])
        proj = jnp.einsum('bsd,de->bse', h, w_in[l])
        q_a = proj[..., :o1].reshape(B, S, SWA_HEADS, HEAD_DIM)
        k_a = proj[..., o1:o2].reshape(B, S, SWA_KV_HEADS, HEAD_DIM)
        v_a = proj[..., o2:o3].reshape(B, S, SWA_KV_HEADS, HEAD_DIM)
        q_b = proj[..., o3:o4].reshape(B, S, SB_HEADS, HEAD_DIM)
        k_b = proj[..., o4:o5].reshape(B, S, SB_HEADS, HEAD_DIM)
        v_b = proj[..., o5:].reshape(B, S, SB_HEADS, HEAD_DIM)
        q_a = rmsnorm(q_a, q_norm_g[l])
        k_a = rmsnorm(k_a, k_norm_g[l])
        o_a = swa_attention(q_a, k_a, v_a, sinks[l], rel_bias)
        o_b = stick_breaking_attention(q_b, k_b, v_b)
        mix = jnp.concatenate([rmsnorm(o_a, swa_out_g[l]), rmsnorm(o_b, sb_out_g[l])], axis=-1)
        x = x + jnp.einsum('bse,ed->bsd', mix, w_out[l])
        h = rmsnorm(x, norm_mlp_g[l])
        u = jnp.einsum('bsd,df->bsf', h, w_up[l])
        x = x + jnp.einsum('bsf,fd->bsd', jnp.square(jax.nn.relu(u)), w_down[l])
    return x
```

```python
import functools
import math

import numpy as np
import jax
import jax.numpy as jnp
from jax import lax
from jax.experimental import pallas as pl
from jax.experimental.pallas import tpu as pltpu

HEAD_DIM = 64
LANES = 128
WINDOW = 128
BLOCK = 128
N_BUCKETS = 32
MAX_DISTANCE = 128
EPS = 1e-6
MASK_VALUE = -1e30
EXP_ZERO_BELOW = -104.0
VMEM_LIMIT_BYTES = 56 * 1024 * 1024

F32 = jnp.float32
BF16 = jnp.bfloat16


def _rms_scale(x):
    return lax.rsqrt(jnp.mean(x * x, axis=-1, keepdims=True) + EPS)


def _norm_matmul_kernel(x_ref, g_ref, w_ref, o_ref, h_ref):
    @pl.when(pl.program_id(1) == 0)
    def _():
        x = x_ref[...]
        h_ref[...] = (x * _rms_scale(x) * g_ref[...]).astype(h_ref.dtype)

    o_ref[...] = jnp.dot(h_ref[...], w_ref[...],
                         preferred_element_type=F32).astype(o_ref.dtype)


def _norm_matmul(x, g, w, *, tm, tn):
    t, d = x.shape
    n = w.shape[1]
    return pl.pallas_call(
        _norm_matmul_kernel,
        grid=(t // tm, n // tn),
        in_specs=[
            pl.BlockSpec((tm, d), lambda i, j: (i, 0)),
            pl.BlockSpec((1, d), lambda i, j: (0, 0)),
            pl.BlockSpec((d, tn), lambda i, j: (0, j)),
        ],
        out_specs=pl.BlockSpec((tm, tn), lambda i, j: (i, j)),
        out_shape=jax.ShapeDtypeStruct((t, n), BF16),
        scratch_shapes=[pltpu.VMEM((tm, d), BF16)],
        compiler_params=pltpu.CompilerParams(
            dimension_semantics=("parallel", "arbitrary"),
            vmem_limit_bytes=VMEM_LIMIT_BYTES),
        name="norm_in_proj",
    )(x, g, w)


def _t5_bucket_np(dist):
    max_exact = N_BUCKETS // 2
    d = np.maximum(dist, 0)
    ratio = np.maximum(d, 1).astype(np.float32) / max_exact
    large = max_exact + (np.log(ratio) / math.log(MAX_DISTANCE / max_exact)
                         * (N_BUCKETS - max_exact)).astype(np.int32)
    large = np.minimum(large, N_BUCKETS - 1)
    return np.where(d < max_exact, d, large).astype(np.int32)


def _bias_table_kernel(rb_ref, bucket_ref, o_ref):
    h = pl.program_id(0)
    bucket = bucket_ref[...]
    acc = jnp.full(bucket.shape, MASK_VALUE, F32)
    for b in range(N_BUCKETS):
        acc = jnp.where(bucket == b, rb_ref[b, h], acc)
    o_ref[0] = acc


def _bias_table(rel_bias):
    n_heads = rel_bias.shape[1]
    qi = np.arange(BLOCK)[:, None]
    kj = np.arange(2 * BLOCK)[None, :]
    dist = qi + BLOCK - kj
    in_window = (dist >= 0) & (dist < WINDOW)
    bucket = np.where(in_window, _t5_bucket_np(dist), -1).astype(np.int32)
    return pl.pallas_call(
        _bias_table_kernel,
        grid=(n_heads,),
        in_specs=[
            pl.BlockSpec(memory_space=pltpu.SMEM),
            pl.BlockSpec((BLOCK, 2 * BLOCK), lambda h: (0, 0)),
        ],
        out_specs=pl.BlockSpec((1, BLOCK, 2 * BLOCK), lambda h: (h, 0, 0)),
        out_shape=jax.ShapeDtypeStruct((n_heads, BLOCK, 2 * BLOCK), F32),
        name="t5_bias_table",
    )(rel_bias.astype(F32), jnp.asarray(bucket))


def _head_sumsq(xf, ones_blockdiag):
    return jnp.dot((xf * xf).astype(BF16), ones_blockdiag, preferred_element_type=F32)


def _dup_halves(xf, lo_half):
    xr = pltpu.roll(xf, HEAD_DIM, axis=1)
    return jnp.where(lo_half, xf, xr), jnp.where(lo_half, xr, xf)


def _swa_kernel(sink_ref, q_ref, kc_ref, kp_ref, vc_ref, vp_ref, bias_ref,
                gq_ref, gk_ref, ones_ref, o_ref, *, heads_per_step):
    kvp = pl.program_id(1)
    n = pl.program_id(2)
    tq = q_ref.shape[0]
    lo_half = lax.broadcasted_iota(jnp.int32, (1, LANES), 1) < HEAD_DIM
    ones_bd = ones_ref[...]

    kf = jnp.concatenate([kp_ref[...], kc_ref[...]], axis=0).astype(F32)
    kn = kf * lax.rsqrt(_head_sumsq(kf, ones_bd) * (1.0 / HEAD_DIM) + EPS) * gk_ref[...]
    kk = [a.astype(BF16) for a in _dup_halves(kn, lo_half)]
    vf = jnp.concatenate([vp_ref[...], vc_ref[...]], axis=0).astype(F32)
    vv = [a.astype(BF16) for a in _dup_halves(vf, lo_half)]

    col = lax.broadcasted_iota(jnp.int32, (1, 2 * tq), 1)
    key_ok = jnp.logical_or(col >= tq, n > 0)
    scale = 1.0 / math.sqrt(HEAD_DIM)

    n_pairs = heads_per_step // 2
    for p in range(n_pairs):
        qf = q_ref[:, p * LANES:(p + 1) * LANES].astype(F32)
        qn = qf * lax.rsqrt(_head_sumsq(qf, ones_bd) * (1.0 / HEAD_DIM) + EPS) * (gq_ref[...] * scale)
        c = (2 * p) // (heads_per_step // 2)
        outs = []
        for half in range(2):
            h = 2 * p + half
            sel = lo_half if half == 0 else jnp.logical_not(lo_half)
            qh = jnp.where(sel, qn, 0.0).astype(BF16)
            s = lax.dot_general(qh, kk[c], (((1,), (1,)), ((), ())), preferred_element_type=F32)
            s = jnp.where(key_ok, s + bias_ref[h], MASK_VALUE)
            sink = sink_ref[kvp * heads_per_step + h]
            m = jnp.maximum(jnp.max(s, axis=-1, keepdims=True), sink)
            e = jnp.exp(s - m)
            denom = jnp.sum(e, axis=-1, keepdims=True) + jnp.exp(sink - m)
            pr = (e * (1.0 / denom)).astype(BF16)
            outs.append(jnp.dot(pr, vv[c], preferred_element_type=F32))
        o_ref[:, p * LANES:(p + 1) * LANES] = jnp.where(lo_half, outs[0], outs[1]).astype(o_ref.dtype)


def _swa_attention(proj, sinks, bias, gq2, gk2, *, batch, seq, n_heads, n_kv_heads, k_col, v_col):
    t = proj.shape[0]
    nb = seq // BLOCK
    kv_pairs = n_kv_heads // 2
    heads_per_step = n_heads // kv_pairs
    qw = heads_per_step * HEAD_DIM
    kcb = k_col // LANES
    vcb = v_col // LANES
    ones_bd = jnp.asarray(np.kron(np.eye(2), np.ones((HEAD_DIM, HEAD_DIM))), BF16)

    cur = lambda cb: (lambda b, g, n: (b * nb + n, cb + g))
    prev = lambda cb: (lambda b, g, n: (b * nb + jnp.maximum(n - 1, 0), cb + g))
    return pl.pallas_call(
        functools.partial(_swa_kernel, heads_per_step=heads_per_step),
        grid=(batch, kv_pairs, nb),
        in_specs=[
            pl.BlockSpec(memory_space=pltpu.SMEM),
            pl.BlockSpec((BLOCK, qw), lambda b, g, n: (b * nb + n, g)),
            pl.BlockSpec((BLOCK, LANES), cur(kcb)),
            pl.BlockSpec((BLOCK, LANES), prev(kcb)),
            pl.BlockSpec((BLOCK, LANES), cur(vcb)),
            pl.BlockSpec((BLOCK, LANES), prev(vcb)),
            pl.BlockSpec((heads_per_step, BLOCK, 2 * BLOCK), lambda b, g, n: (g, 0, 0)),
            pl.BlockSpec((1, LANES), lambda b, g, n: (0, 0)),
            pl.BlockSpec((1, LANES), lambda b, g, n: (0, 0)),
            pl.BlockSpec((LANES, LANES), lambda b, g, n: (0, 0)),
        ],
        out_specs=pl.BlockSpec((BLOCK, qw), lambda b, g, n: (b * nb + n, g)),
        out_shape=jax.ShapeDtypeStruct((t, n_heads * HEAD_DIM), BF16),
        compiler_params=pltpu.CompilerParams(
            dimension_semantics=("parallel", "parallel", "parallel"),
            vmem_limit_bytes=VMEM_LIMIT_BYTES),
        name="swa_attention",
    )(sinks, proj, proj, proj, proj, proj, bias, gq2, gk2, ones_bd)


def _sb_kernel(q_ref, k_ref, v_ref, u_ref, o_ref, acc_ref, carry_ref, *, tb):
    qi = pl.program_id(2)
    lane = lax.broadcasted_iota(jnp.int32, (1, LANES), 1)
    lo_half = lane < HEAD_DIM
    scale = 1.0 / math.sqrt(HEAD_DIM)
    q2 = q_ref[...].astype(F32) * scale
    qh = [jnp.where(lo_half, q2, 0.0).astype(BF16),
          jnp.where(lo_half, 0.0, q2).astype(BF16)]
    u2 = u_ref[...]
    row = lax.broadcasted_iota(jnp.int32, (tb, tb), 0)
    colk = lax.broadcasted_iota(jnp.int32, (tb, tb), 1)
    below_diag = colk < row

    acc_ref[...] = jnp.zeros_like(acc_ref)
    carry_ref[...] = jnp.zeros_like(carry_ref)

    def process(kb, masked):
        start = pl.multiple_of(kb * tb, tb)
        k2 = k_ref[pl.ds(start, tb), :]
        v2 = v_ref[pl.ds(start, tb), :]
        worst = None
        for h in range(2):
            z = lax.dot_general(qh[h], k2, (((1,), (1,)), ((), ())), preferred_element_type=F32)
            soft = jnp.log(1.0 + jnp.exp(-jnp.abs(z)))
            log_beta = jnp.minimum(z, 0.0) - soft
            log_1m = log_beta - z
            if masked:
                log_1m = jnp.where(below_diag, log_1m, 0.0)
            hi = log_1m.astype(BF16)
            lo = (log_1m - hi.astype(F32)).astype(BF16)
            cs = (jnp.dot(hi, u2, preferred_element_type=F32)
                  + jnp.dot(lo, u2, preferred_element_type=F32))
            carry = carry_ref[h]
            w = jnp.exp(log_beta + cs[:, :tb] + carry)
            if masked:
                w = jnp.where(below_diag, w, 0.0)
            acc_ref[h] += jnp.dot(w.astype(BF16), v2, preferred_element_type=F32)
            carry = carry + cs[:, tb:]
            carry_ref[h] = carry
            top = jnp.max(carry)
            worst = top if worst is None else jnp.maximum(worst, top)
        return worst

    worst0 = process(qi, True)

    def cond(state):
        kb, worst = state
        return jnp.logical_and(kb >= 0, worst >= EXP_ZERO_BELOW)

    def body(state):
        kb, _ = state
        return kb - 1, process(kb, False)

    lax.while_loop(cond, body, (qi - 1, worst0))
    o_ref[...] = jnp.where(lo_half, acc_ref[0], acc_ref[1]).astype(o_ref.dtype)


def _sb_attention(proj, *, batch, seq, n_heads, q_col, k_col, v_col, tb):
    t = proj.shape[0]
    nq = seq // tb
    pairs = n_heads // 2
    qcb, kcb, vcb = q_col // LANES, k_col // LANES, v_col // LANES
    tri = np.arange(tb)[:, None] > np.arange(tb)[None, :]
    u2 = jnp.asarray(np.concatenate([tri, np.ones((tb, tb), bool)], axis=1), BF16)
    return pl.pallas_call(
        functools.partial(_sb_kernel, tb=tb),
        grid=(batch, pairs, nq),
        in_specs=[
            pl.BlockSpec((tb, LANES), lambda b, p, i: (b * nq + i, qcb + p)),
            pl.BlockSpec((seq, LANES), lambda b, p, i: (b, kcb + p)),
            pl.BlockSpec((seq, LANES), lambda b, p, i: (b, vcb + p)),
            pl.BlockSpec((tb, 2 * tb), lambda b, p, i: (0, 0)),
        ],
        out_specs=pl.BlockSpec((tb, LANES), lambda b, p, i: (b * nq + i, p)),
        out_shape=jax.ShapeDtypeStruct((t, n_heads * HEAD_DIM), BF16),
        scratch_shapes=[pltpu.VMEM((2, tb, LANES), F32), pltpu.VMEM((2, tb, tb), F32)],
        compiler_params=pltpu.CompilerParams(
            dimension_semantics=("parallel", "parallel", "parallel"),
            vmem_limit_bytes=VMEM_LIMIT_BYTES),
        name="sb_attention",
    )(proj, proj, proj, u2)


def _out_proj_kernel(oa_ref, ob_ref, ga_ref, gb_ref, w_ref, x_ref, o_ref):
    def normed(o_r, g_r):
        o = o_r[...].astype(F32)
        return (o * _rms_scale(o) * g_r[...]).astype(BF16)

    wa = oa_ref.shape[1]
    acc = jnp.dot(normed(oa_ref, ga_ref), w_ref[:wa, :], preferred_element_type=F32)
    acc += jnp.dot(normed(ob_ref, gb_ref), w_ref[wa:, :], preferred_element_type=F32)
    o_ref[...] = x_ref[...] + acc


def _out_proj(o_a, o_b, ga, gb, w, x, *, tm):
    t, d = x.shape
    wa, wb = o_a.shape[1], o_b.shape[1]
    return pl.pallas_call(
        _out_proj_kernel,
        grid=(t // tm,),
        in_specs=[
            pl.BlockSpec((tm, wa), lambda i: (i, 0)),
            pl.BlockSpec((tm, wb), lambda i: (i, 0)),
            pl.BlockSpec((1, wa), lambda i: (0, 0)),
            pl.BlockSpec((1, wb), lambda i: (0, 0)),
            pl.BlockSpec((wa + wb, d), lambda i: (0, 0)),
            pl.BlockSpec((tm, d), lambda i: (i, 0)),
        ],
        out_specs=pl.BlockSpec((tm, d), lambda i: (i, 0)),
        out_shape=jax.ShapeDtypeStruct((t, d), F32),
        compiler_params=pltpu.CompilerParams(
            dimension_semantics=("parallel",),
            vmem_limit_bytes=VMEM_LIMIT_BYTES),
        name="out_proj_residual",
    )(o_a, o_b, ga, gb, w, x)


def _mlp_kernel(x_ref, g_ref, wu_ref, wd_ref, o_ref, h_ref, acc_ref):
    f = pl.program_id(1)

    @pl.when(f == 0)
    def _():
        x = x_ref[...]
        h_ref[...] = (x * _rms_scale(x) * g_ref[...]).astype(h_ref.dtype)
        acc_ref[...] = jnp.zeros_like(acc_ref)

    u = jnp.dot(h_ref[...], wu_ref[...], preferred_element_type=F32)
    u = jnp.maximum(u, 0.0)
    acc_ref[...] += jnp.dot((u * u).astype(BF16), wd_ref[...], preferred_element_type=F32)

    @pl.when(f == pl.num_programs(1) - 1)
    def _():
        o_ref[...] = x_ref[...] + acc_ref[...]


def _mlp(x, g, w_up, w_down, *, tm, tf):
    t, d = x.shape
    ff = w_up.shape[1]
    return pl.pallas_call(
        _mlp_kernel,
        grid=(t // tm, ff // tf),
        in_specs=[
            pl.BlockSpec((tm, d), lambda i, f: (i, 0)),
            pl.BlockSpec((1, d), lambda i, f: (0, 0)),
            pl.BlockSpec((d, tf), lambda i, f: (0, f)),
            pl.BlockSpec((tf, d), lambda i, f: (f, 0)),
        ],
        out_specs=pl.BlockSpec((tm, d), lambda i, f: (i, 0)),
        out_shape=jax.ShapeDtypeStruct((t, d), F32),
        scratch_shapes=[pltpu.VMEM((tm, d), BF16), pltpu.VMEM((tm, d), F32)],
        compiler_params=pltpu.CompilerParams(
            dimension_semantics=("parallel", "arbitrary"),
            vmem_limit_bytes=VMEM_LIMIT_BYTES),
        name="mlp_residual",
    )(x, g, w_up, w_down)


def _tile(total, preferred):
    if total <= preferred:
        return total
    for cand in range(preferred, 0, -LANES):
        if total % cand == 0:
            return cand
    return total


def kernel(x, norm_attn_g, w_in, q_norm_g, k_norm_g, sinks, rel_bias, swa_out_g, sb_out_g,
           w_out, norm_mlp_g, w_up, w_down):
    batch, seq, d_model = x.shape
    depth = w_in.shape[0]
    swa_heads = sinks.shape[1]
    swa_q_w = swa_heads * HEAD_DIM
    sb_w = sb_out_g.shape[1]
    sb_heads = sb_w // HEAD_DIM
    d_in = w_in.shape[2]
    swa_kv_w = (d_in - swa_q_w - 3 * sb_w) // 2
    swa_kv_heads = swa_kv_w // HEAD_DIM
    o1 = swa_q_w
    o2 = o1 + swa_kv_w
    o3 = o2 + swa_kv_w
    o4 = o3 + sb_w
    o5 = o4 + sb_w
    assert seq % BLOCK == 0 and swa_kv_heads % 2 == 0 and sb_heads % 2 == 0
    assert swa_heads % swa_kv_heads == 0 and (swa_heads // swa_kv_heads) % 2 == 0

    t = batch * seq
    tm = _tile(t, 512)
    tn_in = _tile(d_in, 1536)
    tf = _tile(w_up.shape[2], 512)

    xt = x.reshape(t, d_model).astype(F32)
    bias = _bias_table(rel_bias)
    row = lambda v: v.reshape(1, -1).astype(F32)

    for l in range(depth):
        proj = _norm_matmul(xt, row(norm_attn_g[l]), w_in[l].astype(BF16), tm=tm, tn=tn_in)
        gq2 = row(jnp.tile(q_norm_g[l], LANES // HEAD_DIM))
        gk2 = row(jnp.tile(k_norm_g[l], LANES // HEAD_DIM))
        o_a = _swa_attention(proj, sinks[l].astype(F32), bias, gq2, gk2, batch=batch, seq=seq,
                             n_heads=swa_heads, n_kv_heads=swa_kv_heads, k_col=o1, v_col=o2)
        o_b = _sb_attention(proj, batch=batch, seq=seq, n_heads=sb_heads,
                            q_col=o3, k_col=o4, v_col=o5, tb=BLOCK)
        xt = _out_proj(o_a, o_b, row(swa_out_g[l]), row(sb_out_g[l]), w_out[l].astype(BF16), xt, tm=tm)
        xt = _mlp(xt, row(norm_mlp_g[l]), w_up[l].astype(BF16), w_down[l].astype(BF16), tm=tm, tf=tf)
    return xt.reshape(batch, seq, d_model).astype(x.dtype)
```

```python
import functools
import math

import numpy as np
import jax
import jax.numpy as jnp
from jax import lax
from jax.experimental import pallas as pl
from jax.experimental.pallas import tpu as pltpu

HEAD_DIM = 64
LANES = 128
WINDOW = 128
BLOCK = 128
N_BUCKETS = 32
MAX_DISTANCE = 128
EPS = 1e-6
MASK_VALUE = -1e30
EXP_ZERO_BELOW = -104.0
VMEM_LIMIT_BYTES = 56 * 1024 * 1024
SB_GROUP = 4
SWA_GROUP = 2

F32 = jnp.float32
BF16 = jnp.bfloat16


def _rms_scale(x):
    return lax.rsqrt(jnp.mean(x * x, axis=-1, keepdims=True) + EPS)


def _norm_matmul_kernel(x_ref, g_ref, w_ref, o_ref, h_ref):
    @pl.when(pl.program_id(1) == 0)
    def _():
        x = x_ref[...]
        h_ref[...] = (x * _rms_scale(x) * g_ref[...]).astype(h_ref.dtype)

    o_ref[...] = jnp.dot(h_ref[...], w_ref[...],
                         preferred_element_type=F32).astype(o_ref.dtype)


def _norm_matmul(x, g, w, *, tm, tn):
    t, d = x.shape
    n = w.shape[1]
    return pl.pallas_call(
        _norm_matmul_kernel,
        grid=(t // tm, n // tn),
        in_specs=[
            pl.BlockSpec((tm, d), lambda i, j: (i, 0)),
            pl.BlockSpec((1, d), lambda i, j: (0, 0)),
            pl.BlockSpec((d, tn), lambda i, j: (0, j)),
        ],
        out_specs=pl.BlockSpec((tm, tn), lambda i, j: (i, j)),
        out_shape=jax.ShapeDtypeStruct((t, n), BF16),
        scratch_shapes=[pltpu.VMEM((tm, d), BF16)],
        compiler_params=pltpu.CompilerParams(
            dimension_semantics=("parallel", "arbitrary"),
            vmem_limit_bytes=VMEM_LIMIT_BYTES),
        name="norm_in_proj",
    )(x, g, w)


def _t5_bucket_np(dist):
    max_exact = N_BUCKETS // 2
    d = np.maximum(dist, 0)
    ratio = np.maximum(d, 1).astype(np.float32) / max_exact
    large = max_exact + (np.log(ratio) / math.log(MAX_DISTANCE / max_exact)
                         * (N_BUCKETS - max_exact)).astype(np.int32)
    large = np.minimum(large, N_BUCKETS - 1)
    return np.where(d < max_exact, d, large).astype(np.int32)


def _bias_table_kernel(rb_ref, bucket_ref, o_ref):
    h = pl.program_id(0)
    bucket = bucket_ref[...]
    acc = jnp.full(bucket.shape, MASK_VALUE, F32)
    for b in range(N_BUCKETS):
        acc = jnp.where(bucket == b, rb_ref[b, h], acc)
    o_ref[0] = acc


def _bias_table(rel_bias, gqa):
    n_heads = rel_bias.shape[1]
    qi = np.arange(BLOCK)[:, None]
    kj = np.arange(2 * BLOCK)[None, :]
    dist = qi + BLOCK - kj
    in_window = (dist >= 0) & (dist < WINDOW)
    bucket = np.where(in_window, _t5_bucket_np(dist), -1).astype(np.int32)
    return pl.pallas_call(
        _bias_table_kernel,
        grid=(n_heads,),
        in_specs=[
            pl.BlockSpec(memory_space=pltpu.SMEM),
            pl.BlockSpec((BLOCK, 2 * BLOCK), lambda h: (0, 0)),
        ],
        out_specs=pl.BlockSpec((1, BLOCK, 2 * BLOCK), lambda h: (h // gqa, (h % gqa) // 2, h % 2)),
        out_shape=jax.ShapeDtypeStruct((n_heads // gqa, (gqa // 2) * BLOCK, 4 * BLOCK), F32),
        name="t5_bias_table",
    )(rel_bias.astype(F32), jnp.asarray(bucket))


def _head_sumsq(xf, ones_blockdiag):
    return jnp.dot((xf * xf).astype(BF16), ones_blockdiag, preferred_element_type=F32)


def _lane_half_copies(xf, lo_half):
    xr = pltpu.roll(xf, HEAD_DIM, axis=1)
    zero = jnp.zeros_like(xf)
    lo = lambda x: jnp.where(lo_half, x, zero).astype(BF16)
    hi = lambda x: jnp.where(lo_half, zero, x).astype(BF16)
    return [(lo(xf), hi(xr)), (lo(xr), hi(xf))]


def _swa_kernel(sink_ref, q_ref, kg_ref, kp_ref, vg_ref, vp_ref, bias_ref,
                gq_ref, gk_ref, ones_bd_ref, ones_cat_ref, o_ref, *, gqa, group):
    kvp = pl.program_id(1)
    step = pl.program_id(2)
    ppk = gqa // 2
    lo_half = lax.broadcasted_iota(jnp.int32, (1, LANES), 1) < HEAD_DIM
    ones_bd = ones_bd_ref[...]
    ones_cat = ones_cat_ref[...]
    nt = (((1,), (1,)), ((), ()))
    scale = 1.0 / math.sqrt(HEAD_DIM)

    kf = jnp.concatenate([kp_ref[...], kg_ref[...]], axis=0).astype(F32)
    kn = kf * lax.rsqrt(_head_sumsq(kf, ones_bd) * (1.0 / HEAD_DIM) + EPS) * gk_ref[...]
    k_copies = _lane_half_copies(kn, lo_half)
    v_copies = _lane_half_copies(jnp.concatenate([vp_ref[...], vg_ref[...]], axis=0).astype(F32), lo_half)

    col = lax.broadcasted_iota(jnp.int32, (1, 2 * BLOCK), 1)
    row_top = lax.broadcasted_iota(jnp.int32, (ppk * BLOCK, 1), 0)

    for j in range(group):
        r0 = j * BLOCK
        key_ok = jnp.logical_or(col >= BLOCK, step * group + j > 0)
        qn = []
        for r in range(2 * ppk):
            qf = q_ref[r0:r0 + BLOCK, r * LANES:(r + 1) * LANES].astype(F32)
            qn.append((qf * lax.rsqrt(_head_sumsq(qf, ones_bd) * (1.0 / HEAD_DIM) + EPS)
                       * (gq_ref[...] * scale)).astype(BF16))
        for c in range(2):
            k_lo, k_hi = k_copies[c]
            v_lo, v_hi = v_copies[c]
            kcat = jnp.concatenate([k_lo[r0:r0 + 2 * BLOCK], k_hi[r0:r0 + 2 * BLOCK]], axis=0)
            vcat = jnp.concatenate([v_lo[r0:r0 + 2 * BLOCK], v_hi[r0:r0 + 2 * BLOCK]], axis=0)
            qs = jnp.concatenate(qn[c * ppk:(c + 1) * ppk], axis=0)
            z = lax.dot_general(qs, kcat, nt, preferred_element_type=F32)
            es, sink_terms = [], []
            for h in range(2):
                sl = slice(h * 2 * BLOCK, (h + 1) * 2 * BLOCK)
                s = jnp.where(key_ok, z[:, sl] + bias_ref[c, :, sl], MASK_VALUE)
                head0 = kvp * 2 * gqa + c * gqa + h
                sink = jnp.full((ppk * BLOCK, 1), sink_ref[head0], F32)
                for r in range(1, ppk):
                    sink = jnp.where(row_top >= r * BLOCK, sink_ref[head0 + 2 * r], sink)
                m = jnp.maximum(jnp.max(s, axis=-1, keepdims=True), sink)
                es.append(jnp.exp(s - m).astype(BF16))
                sink_terms.append(jnp.exp(sink - m))
            e = jnp.concatenate(es, axis=1)
            num = jnp.dot(e, vcat, preferred_element_type=F32)
            den = jnp.dot(e, ones_cat, preferred_element_type=F32)
            den = den + jnp.where(lo_half, sink_terms[0], sink_terms[1])
            out = (num * (1.0 / den)).astype(o_ref.dtype)
            for r in range(ppk):
                lane0 = (c * ppk + r) * LANES
                o_ref[r0:r0 + BLOCK, lane0:lane0 + LANES] = out[r * BLOCK:(r + 1) * BLOCK]


def _swa_attention(proj, sinks, bias, gq2, gk2, *, batch, seq, n_heads, n_kv_heads, k_col, v_col, group):
    t = proj.shape[0]
    nb = seq // BLOCK
    ns = nb // group
    kv_pairs = n_kv_heads // 2
    gqa = n_heads // n_kv_heads
    qw = 2 * gqa * HEAD_DIM
    kcb = k_col // LANES
    vcb = v_col // LANES
    ones_bd = jnp.asarray(np.kron(np.eye(2), np.ones((HEAD_DIM, HEAD_DIM))), BF16)
    half = np.arange(LANES)[None, :] < HEAD_DIM
    top = np.arange(4 * BLOCK)[:, None] < 2 * BLOCK
    ones_cat = jnp.asarray(top == half, BF16)

    cur = lambda cb: (lambda b, g, i: (b * ns + i, cb + g))
    prev = lambda cb: (lambda b, g, i: (b * nb + jnp.maximum(i * group - 1, 0), cb + g))
    return pl.pallas_call(
        functools.partial(_swa_kernel, gqa=gqa, group=group),
        grid=(batch, kv_pairs, ns),
        in_specs=[
            pl.BlockSpec(memory_space=pltpu.SMEM),
            pl.BlockSpec((group * BLOCK, qw), lambda b, g, i: (b * ns + i, g)),
            pl.BlockSpec((group * BLOCK, LANES), cur(kcb)),
            pl.BlockSpec((BLOCK, LANES), prev(kcb)),
            pl.BlockSpec((group * BLOCK, LANES), cur(vcb)),
            pl.BlockSpec((BLOCK, LANES), prev(vcb)),
            pl.BlockSpec((2, (gqa // 2) * BLOCK, 4 * BLOCK), lambda b, g, i: (g, 0, 0)),
            pl.BlockSpec((1, LANES), lambda b, g, i: (0, 0)),
            pl.BlockSpec((1, LANES), lambda b, g, i: (0, 0)),
            pl.BlockSpec((LANES, LANES), lambda b, g, i: (0, 0)),
            pl.BlockSpec((4 * BLOCK, LANES), lambda b, g, i: (0, 0)),
        ],
        out_specs=pl.BlockSpec((group * BLOCK, qw), lambda b, g, i: (b * ns + i, g)),
        out_shape=jax.ShapeDtypeStruct((t, n_heads * HEAD_DIM), BF16),
        compiler_params=pltpu.CompilerParams(
            dimension_semantics=("parallel", "parallel", "parallel"),
            vmem_limit_bytes=VMEM_LIMIT_BYTES),
        name="swa_attention",
    )(sinks, proj, proj, proj, proj, proj, bias, gq2, gk2, ones_bd, ones_cat)


def _sb_kernel(q_ref, k_ref, v_ref, u_ref, o_ref, acc_ref, carry_ref, *, tb, group):
    step = pl.program_id(2)
    lo_half = lax.broadcasted_iota(jnp.int32, (1, LANES), 1) < HEAD_DIM
    scale = 1.0 / math.sqrt(HEAD_DIM)
    uu = u_ref[...]
    row = lax.broadcasted_iota(jnp.int32, (tb, tb), 0)
    colk = lax.broadcasted_iota(jnp.int32, (tb, tb), 1)
    below_diag = colk < row
    nt = (((1,), (1,)), ((), ()))

    acc_ref[...] = jnp.zeros_like(acc_ref)
    carry_ref[...] = jnp.zeros_like(carry_ref)

    def split_heads(x2):
        zero = jnp.zeros_like(x2)
        return jnp.concatenate([jnp.where(lo_half, x2, zero), jnp.where(lo_half, zero, x2)], axis=0)

    def process(d, masked):
        worst = None
        for i in range(group):
            kb = step * group + i - d
            active = kb >= 0
            start = pl.multiple_of(jnp.maximum(kb, 0) * tb, tb)
            kcat = split_heads(k_ref[pl.ds(start, tb), :])
            vcat = split_heads(v_ref[pl.ds(start, tb), :])
            q2 = (q_ref[i * tb:(i + 1) * tb, :].astype(F32) * scale).astype(BF16)
            z2 = lax.dot_general(q2, kcat, nt, preferred_element_type=F32)
            ws = []
            for h in range(2):
                z = z2[:, h * tb:(h + 1) * tb]
                soft = jnp.log(1.0 + jnp.exp(-jnp.abs(z)))
                log_beta = jnp.minimum(z, 0.0) - soft
                log_1m = log_beta - z
                if masked:
                    log_1m = jnp.where(below_diag, log_1m, 0.0)
                hi = log_1m.astype(BF16)
                lo = (log_1m - hi.astype(F32)).astype(BF16)
                cs = jnp.dot(jnp.concatenate([hi, lo], axis=1), uu, preferred_element_type=F32)
                carry = jnp.where(active, carry_ref[i, h], MASK_VALUE)
                w = jnp.exp(log_beta + cs[:, :tb] + carry)
                if masked:
                    w = jnp.where(below_diag, w, 0.0)
                ws.append(w.astype(BF16))
                carry = carry + cs[:, tb:]
                carry_ref[i, h] = carry
                top = jnp.max(carry)
                worst = top if worst is None else jnp.maximum(worst, top)
            acc_ref[i] += jnp.dot(jnp.concatenate(ws, axis=1), vcat, preferred_element_type=F32)
        return worst

    worst0 = process(0, True)
    last_block = step * group + group - 1

    def cond(state):
        d, worst = state
        return jnp.logical_and(d <= last_block, worst >= EXP_ZERO_BELOW)

    def body(state):
        d, _ = state
        return d + 1, process(d, False)

    lax.while_loop(cond, body, (jnp.int32(1), worst0))
    for i in range(group):
        o_ref[i * tb:(i + 1) * tb, :] = acc_ref[i].astype(o_ref.dtype)


def _sb_attention(proj, *, batch, seq, n_heads, q_col, k_col, v_col, tb, group):
    t = proj.shape[0]
    nq = seq // (tb * group)
    pairs = n_heads // 2
    qcb, kcb, vcb = q_col // LANES, k_col // LANES, v_col // LANES
    tri = np.arange(tb)[:, None] > np.arange(tb)[None, :]
    u2 = np.concatenate([tri, np.ones((tb, tb), bool)], axis=1)
    uu = jnp.asarray(np.concatenate([u2, u2], axis=0), BF16)
    return pl.pallas_call(
        functools.partial(_sb_kernel, tb=tb, group=group),
        grid=(batch, pairs, nq),
        in_specs=[
            pl.BlockSpec((tb * group, LANES), lambda b, p, i: (b * nq + i, qcb + p)),
            pl.BlockSpec((seq, LANES), lambda b, p, i: (b, kcb + p)),
            pl.BlockSpec((seq, LANES), lambda b, p, i: (b, vcb + p)),
            pl.BlockSpec((2 * tb, 2 * tb), lambda b, p, i: (0, 0)),
        ],
        out_specs=pl.BlockSpec((tb * group, LANES), lambda b, p, i: (b * nq + i, p)),
        out_shape=jax.ShapeDtypeStruct((t, n_heads * HEAD_DIM), BF16),
        scratch_shapes=[pltpu.VMEM((group, tb, LANES), F32), pltpu.VMEM((group, 2, tb, tb), F32)],
        compiler_params=pltpu.CompilerParams(
            dimension_semantics=("parallel", "parallel", "parallel"),
            vmem_limit_bytes=VMEM_LIMIT_BYTES),
        name="sb_attention",
    )(proj, proj, proj, uu)


def _out_proj_kernel(oa_ref, ob_ref, ga_ref, gb_ref, w_ref, x_ref, o_ref):
    def normed(o_r, g_r):
        o = o_r[...].astype(F32)
        return (o * _rms_scale(o) * g_r[...]).astype(BF16)

    wa = oa_ref.shape[1]
    acc = jnp.dot(normed(oa_ref, ga_ref), w_ref[:wa, :], preferred_element_type=F32)
    acc += jnp.dot(normed(ob_ref, gb_ref), w_ref[wa:, :], preferred_element_type=F32)
    o_ref[...] = x_ref[...] + acc


def _out_proj(o_a, o_b, ga, gb, w, x, *, tm):
    t, d = x.shape
    wa, wb = o_a.shape[1], o_b.shape[1]
    return pl.pallas_call(
        _out_proj_kernel,
        grid=(t // tm,),
        in_specs=[
            pl.BlockSpec((tm, wa), lambda i: (i, 0)),
            pl.BlockSpec((tm, wb), lambda i: (i, 0)),
            pl.BlockSpec((1, wa), lambda i: (0, 0)),
            pl.BlockSpec((1, wb), lambda i: (0, 0)),
            pl.BlockSpec((wa + wb, d), lambda i: (0, 0)),
            pl.BlockSpec((tm, d), lambda i: (i, 0)),
        ],
        out_specs=pl.BlockSpec((tm, d), lambda i: (i, 0)),
        out_shape=jax.ShapeDtypeStruct((t, d), F32),
        compiler_params=pltpu.CompilerParams(
            dimension_semantics=("parallel",),
            vmem_limit_bytes=VMEM_LIMIT_BYTES),
        name="out_proj_residual",
    )(o_a, o_b, ga, gb, w, x)


def _mlp_kernel(x_ref, g_ref, wu_ref, wd_ref, o_ref, h_ref, acc_ref):
    f = pl.program_id(1)

    @pl.when(f == 0)
    def _():
        x = x_ref[...]
        h_ref[...] = (x * _rms_scale(x) * g_ref[...]).astype(h_ref.dtype)
        acc_ref[...] = jnp.zeros_like(acc_ref)

    u = jnp.dot(h_ref[...], wu_ref[...], preferred_element_type=F32)
    u = jnp.maximum(u, 0.0)
    acc_ref[...] += jnp.dot((u * u).astype(BF16), wd_ref[...], preferred_element_type=F32)

    @pl.when(f == pl.num_programs(1) - 1)
    def _():
        o_ref[...] = x_ref[...] + acc_ref[...]


def _mlp(x, g, w_up, w_down, *, tm, tf):
    t, d = x.shape
    ff = w_up.shape[1]
    return pl.pallas_call(
        _mlp_kernel,
        grid=(t // tm, ff // tf),
        in_specs=[
            pl.BlockSpec((tm, d), lambda i, f: (i, 0)),
            pl.BlockSpec((1, d), lambda i, f: (0, 0)),
            pl.BlockSpec((d, tf), lambda i, f: (0, f)),
            pl.BlockSpec((tf, d), lambda i, f: (f, 0)),
        ],
        out_specs=pl.BlockSpec((tm, d), lambda i, f: (i, 0)),
        out_shape=jax.ShapeDtypeStruct((t, d), F32),
        scratch_shapes=[pltpu.VMEM((tm, d), BF16), pltpu.VMEM((tm, d), F32)],
        compiler_params=pltpu.CompilerParams(
            dimension_semantics=("parallel", "arbitrary"),
            vmem_limit_bytes=VMEM_LIMIT_BYTES),
        name="mlp_residual",
    )(x, g, w_up, w_down)


def _tile(total, preferred):
    if total <= preferred:
        return total
    for cand in range(preferred, 0, -LANES):
        if total % cand == 0:
            return cand
    return total


def kernel(x, norm_attn_g, w_in, q_norm_g, k_norm_g, sinks, rel_bias, swa_out_g, sb_out_g,
           w_out, norm_mlp_g, w_up, w_down):
    batch, seq, d_model = x.shape
    depth = w_in.shape[0]
    swa_heads = sinks.shape[1]
    swa_q_w = swa_heads * HEAD_DIM
    sb_w = sb_out_g.shape[1]
    sb_heads = sb_w // HEAD_DIM
    d_in = w_in.shape[2]
    swa_kv_w = (d_in - swa_q_w - 3 * sb_w) // 2
    swa_kv_heads = swa_kv_w // HEAD_DIM
    o1 = swa_q_w
    o2 = o1 + swa_kv_w
    o3 = o2 + swa_kv_w
    o4 = o3 + sb_w
    o5 = o4 + sb_w
    assert seq % BLOCK == 0 and swa_kv_heads % 2 == 0 and sb_heads % 2 == 0
    assert swa_heads % swa_kv_heads == 0 and (swa_heads // swa_kv_heads) % 2 == 0

    t = batch * seq
    tm = _tile(t, 512)
    tn_in = _tile(d_in, 1536)
    tf = _tile(w_up.shape[2], 512)
    sb_group = next(g for g in (SB_GROUP, 2, 1) if (seq // BLOCK) % g == 0)

    xt = x.reshape(t, d_model).astype(F32)
    swa_group = next(g for g in (SWA_GROUP, 1) if (seq // BLOCK) % g == 0)
    bias = _bias_table(rel_bias, swa_heads // swa_kv_heads)
    row = lambda v: v.reshape(1, -1).astype(F32)

    for l in range(depth):
        proj = _norm_matmul(xt, row(norm_attn_g[l]), w_in[l].astype(BF16), tm=tm, tn=tn_in)
        gq2 = row(jnp.tile(q_norm_g[l], LANES // HEAD_DIM))
        gk2 = row(jnp.tile(k_norm_g[l], LANES // HEAD_DIM))
        o_a = _swa_attention(proj, sinks[l].astype(F32), bias, gq2, gk2, batch=batch, seq=seq,
                             n_heads=swa_heads, n_kv_heads=swa_kv_heads, k_col=o1, v_col=o2,
                             group=swa_group)
        o_b = _sb_attention(proj, batch=batch, seq=seq, n_heads=sb_heads,
                            q_col=o3, k_col=o4, v_col=o5, tb=BLOCK, group=sb_group)
        xt = _out_proj(o_a, o_b, row(swa_out_g[l]), row(sb_out_g[l]), w_out[l].astype(BF16), xt, tm=tm)
        xt = _mlp(xt, row(norm_mlp_g[l]), w_up[l].astype(BF16), w_down[l].astype(BF16), tm=tm, tf=tf)
    return xt.reshape(batch, seq, d_model).astype(x.dtype)
```

```python
import functools
import math

import numpy as np
import jax
import jax.numpy as jnp
from jax import lax
from jax.experimental import pallas as pl
from jax.experimental.pallas import tpu as pltpu

HEAD_DIM = 64
LANES = 128
WINDOW = 128
BLOCK = 128
N_BUCKETS = 32
MAX_DISTANCE = 128
EPS = 1e-6
MASK_VALUE = -1e30
EXP_ZERO_BELOW = -104.0
VMEM_LIMIT_BYTES = 56 * 1024 * 1024
SB_GROUP = 8
SWA_GROUP = 4

F32 = jnp.float32
BF16 = jnp.bfloat16


def _rms_scale(x):
    return lax.rsqrt(jnp.mean(x * x, axis=-1, keepdims=True) + EPS)


def _norm_matmul_kernel(x_ref, g_ref, w_ref, o_ref, *, tn):
    x = x_ref[...]
    h = (x * _rms_scale(x) * g_ref[...]).astype(BF16)
    for c0 in range(0, o_ref.shape[1], tn):
        o_ref[:, c0:c0 + tn] = jnp.dot(h, w_ref[:, c0:c0 + tn],
                                       preferred_element_type=F32).astype(o_ref.dtype)


def _norm_matmul(x, g, w, *, tm, tn):
    t, d = x.shape
    n = w.shape[1]
    return pl.pallas_call(
        functools.partial(_norm_matmul_kernel, tn=tn),
        grid=(t // tm,),
        in_specs=[
            pl.BlockSpec((tm, d), lambda i: (i, 0)),
            pl.BlockSpec((1, d), lambda i: (0, 0)),
            pl.BlockSpec((d, n), lambda i: (0, 0), pipeline_mode=pl.Buffered(1)),
        ],
        out_specs=pl.BlockSpec((tm, n), lambda i: (i, 0)),
        out_shape=jax.ShapeDtypeStruct((t, n), BF16),
        compiler_params=pltpu.CompilerParams(
            dimension_semantics=("parallel",),
            vmem_limit_bytes=VMEM_LIMIT_BYTES),
        name="norm_in_proj",
    )(x, g, w)


def _t5_bucket_np(dist):
    max_exact = N_BUCKETS // 2
    d = np.maximum(dist, 0)
    ratio = np.maximum(d, 1).astype(np.float32) / max_exact
    large = max_exact + (np.log(ratio) / math.log(MAX_DISTANCE / max_exact)
                         * (N_BUCKETS - max_exact)).astype(np.int32)
    large = np.minimum(large, N_BUCKETS - 1)
    return np.where(d < max_exact, d, large).astype(np.int32)


def _bias_table_kernel(rb_ref, bucket_ref, o_ref):
    h = pl.program_id(0)
    bucket = bucket_ref[...]
    acc = jnp.full(bucket.shape, MASK_VALUE, F32)
    for b in range(N_BUCKETS):
        acc = jnp.where(bucket == b, rb_ref[b, h], acc)
    o_ref[0] = acc


def _bias_table(rel_bias, gqa):
    n_heads = rel_bias.shape[1]
    qi = np.arange(BLOCK)[:, None]
    kj = np.arange(2 * BLOCK)[None, :]
    dist = qi + BLOCK - kj
    in_window = (dist >= 0) & (dist < WINDOW)
    bucket = np.where(in_window, _t5_bucket_np(dist), -1).astype(np.int32)
    return pl.pallas_call(
        _bias_table_kernel,
        grid=(n_heads,),
        in_specs=[
            pl.BlockSpec(memory_space=pltpu.SMEM),
            pl.BlockSpec((BLOCK, 2 * BLOCK), lambda h: (0, 0)),
        ],
        out_specs=pl.BlockSpec((1, BLOCK, 2 * BLOCK), lambda h: (h // gqa, (h % gqa) // 2, h % 2)),
        out_shape=jax.ShapeDtypeStruct((n_heads // gqa, (gqa // 2) * BLOCK, 4 * BLOCK), F32),
        name="t5_bias_table",
    )(rel_bias.astype(F32), jnp.asarray(bucket))


def _head_sumsq(xf, ones_blockdiag):
    return jnp.dot((xf * xf).astype(BF16), ones_blockdiag, preferred_element_type=F32)


def _lane_half_copies(xf, lo_half):
    xr = pltpu.roll(xf, HEAD_DIM, axis=1)
    zero = jnp.zeros_like(xf)
    lo = lambda x: jnp.where(lo_half, x, zero).astype(BF16)
    hi = lambda x: jnp.where(lo_half, zero, x).astype(BF16)
    return [(lo(xf), hi(xr)), (lo(xr), hi(xf))]


def _swa_kernel(sink_ref, q_ref, kg_ref, kp_ref, vg_ref, vp_ref, bias_ref,
                gq_ref, gk_ref, ones_bd_ref, ones_cat_ref, o_ref, *, gqa, group):
    kvp = pl.program_id(1)
    step = pl.program_id(2)
    ppk = gqa // 2
    lo_half = lax.broadcasted_iota(jnp.int32, (1, LANES), 1) < HEAD_DIM
    ones_bd = ones_bd_ref[...]
    ones_cat = ones_cat_ref[...]
    nt = (((1,), (1,)), ((), ()))
    scale = 1.0 / math.sqrt(HEAD_DIM)

    kf = jnp.concatenate([kp_ref[...], kg_ref[...]], axis=0).astype(F32)
    kn = kf * lax.rsqrt(_head_sumsq(kf, ones_bd) * (1.0 / HEAD_DIM) + EPS) * gk_ref[...]
    k_copies = _lane_half_copies(kn, lo_half)
    v_copies = _lane_half_copies(jnp.concatenate([vp_ref[...], vg_ref[...]], axis=0).astype(F32), lo_half)

    col = lax.broadcasted_iota(jnp.int32, (1, 2 * BLOCK), 1)
    row_top = lax.broadcasted_iota(jnp.int32, (ppk * BLOCK, 1), 0)

    for j in range(group):
        r0 = j * BLOCK
        key_ok = jnp.logical_or(col >= BLOCK, step * group + j > 0)
        qn = []
        for r in range(2 * ppk):
            qf = q_ref[r0:r0 + BLOCK, r * LANES:(r + 1) * LANES].astype(F32)
            qn.append((qf * lax.rsqrt(_head_sumsq(qf, ones_bd) * (1.0 / HEAD_DIM) + EPS)
                       * (gq_ref[...] * scale)).astype(BF16))
        for c in range(2):
            k_lo, k_hi = k_copies[c]
            v_lo, v_hi = v_copies[c]
            kcat = jnp.concatenate([k_lo[r0:r0 + 2 * BLOCK], k_hi[r0:r0 + 2 * BLOCK]], axis=0)
            vcat = jnp.concatenate([v_lo[r0:r0 + 2 * BLOCK], v_hi[r0:r0 + 2 * BLOCK]], axis=0)
            qs = jnp.concatenate(qn[c * ppk:(c + 1) * ppk], axis=0)
            z = lax.dot_general(qs, kcat, nt, preferred_element_type=F32)
            es, sink_terms = [], []
            for h in range(2):
                sl = slice(h * 2 * BLOCK, (h + 1) * 2 * BLOCK)
                s = jnp.where(key_ok, z[:, sl] + bias_ref[c, :, sl], MASK_VALUE)
                head0 = kvp * 2 * gqa + c * gqa + h
                sink = jnp.full((ppk * BLOCK, 1), sink_ref[head0], F32)
                for r in range(1, ppk):
                    sink = jnp.where(row_top >= r * BLOCK, sink_ref[head0 + 2 * r], sink)
                m = jnp.maximum(jnp.max(s, axis=-1, keepdims=True), sink)
                es.append(jnp.exp(s - m).astype(BF16))
                sink_terms.append(jnp.exp(sink - m))
            e = jnp.concatenate(es, axis=1)
            num = jnp.dot(e, vcat, preferred_element_type=F32)
            den = jnp.dot(e, ones_cat, preferred_element_type=F32)
            den = den + jnp.where(lo_half, sink_terms[0], sink_terms[1])
            out = (num * (1.0 / den)).astype(o_ref.dtype)
            for r in range(ppk):
                lane0 = (c * ppk + r) * LANES
                o_ref[r0:r0 + BLOCK, lane0:lane0 + LANES] = out[r * BLOCK:(r + 1) * BLOCK]


def _swa_attention(proj, sinks, bias, gq2, gk2, *, batch, seq, n_heads, n_kv_heads, k_col, v_col, group):
    t = proj.shape[0]
    nb = seq // BLOCK
    ns = nb // group
    kv_pairs = n_kv_heads // 2
    gqa = n_heads // n_kv_heads
    qw = 2 * gqa * HEAD_DIM
    kcb = k_col // LANES
    vcb = v_col // LANES
    ones_bd = jnp.asarray(np.kron(np.eye(2), np.ones((HEAD_DIM, HEAD_DIM))), BF16)
    half = np.arange(LANES)[None, :] < HEAD_DIM
    top = np.arange(4 * BLOCK)[:, None] < 2 * BLOCK
    ones_cat = jnp.asarray(top == half, BF16)

    cur = lambda cb: (lambda b, g, i: (b * ns + i, cb + g))
    prev = lambda cb: (lambda b, g, i: (b * nb + jnp.maximum(i * group - 1, 0), cb + g))
    return pl.pallas_call(
        functools.partial(_swa_kernel, gqa=gqa, group=group),
        grid=(batch, kv_pairs, ns),
        in_specs=[
            pl.BlockSpec(memory_space=pltpu.SMEM),
            pl.BlockSpec((group * BLOCK, qw), lambda b, g, i: (b * ns + i, g)),
            pl.BlockSpec((group * BLOCK, LANES), cur(kcb)),
            pl.BlockSpec((BLOCK, LANES), prev(kcb)),
            pl.BlockSpec((group * BLOCK, LANES), cur(vcb)),
            pl.BlockSpec((BLOCK, LANES), prev(vcb)),
            pl.BlockSpec((2, (gqa // 2) * BLOCK, 4 * BLOCK), lambda b, g, i: (g, 0, 0)),
            pl.BlockSpec((1, LANES), lambda b, g, i: (0, 0)),
            pl.BlockSpec((1, LANES), lambda b, g, i: (0, 0)),
            pl.BlockSpec((LANES, LANES), lambda b, g, i: (0, 0)),
            pl.BlockSpec((4 * BLOCK, LANES), lambda b, g, i: (0, 0)),
        ],
        out_specs=pl.BlockSpec((group * BLOCK, qw), lambda b, g, i: (b * ns + i, g)),
        out_shape=jax.ShapeDtypeStruct((t, n_heads * HEAD_DIM), BF16),
        compiler_params=pltpu.CompilerParams(
            dimension_semantics=("parallel", "parallel", "parallel"),
            vmem_limit_bytes=VMEM_LIMIT_BYTES),
        name="swa_attention",
    )(sinks, proj, proj, proj, proj, proj, bias, gq2, gk2, ones_bd, ones_cat)


def _sb_kernel(q_ref, k_ref, v_ref, u_ref, o_ref, acc_ref, carry_ref, *, tb, group):
    step = pl.program_id(2)
    lo_half = lax.broadcasted_iota(jnp.int32, (1, LANES), 1) < HEAD_DIM
    scale = 1.0 / math.sqrt(HEAD_DIM)
    uu = u_ref[...]
    row = lax.broadcasted_iota(jnp.int32, (tb, tb), 0)
    colk = lax.broadcasted_iota(jnp.int32, (tb, tb), 1)
    below_diag = colk < row
    nt = (((1,), (1,)), ((), ()))

    acc_ref[...] = jnp.zeros_like(acc_ref)
    carry_ref[...] = jnp.zeros_like(carry_ref)

    def split_heads(x2):
        zero = jnp.zeros_like(x2)
        return jnp.concatenate([jnp.where(lo_half, x2, zero), jnp.where(lo_half, zero, x2)], axis=0)

    q2s = [(q_ref[i * tb:(i + 1) * tb, :].astype(F32) * scale).astype(BF16) for i in range(group)]

    def process(d, masked):
        actives, vcats, log_betas, xs = [], [], [], []
        for i in range(group):
            kb = step * group + i - d
            actives.append(kb >= 0)
            start = pl.multiple_of(jnp.maximum(kb, 0) * tb, tb)
            kcat = split_heads(k_ref[pl.ds(start, tb), :])
            vcats.append(split_heads(v_ref[pl.ds(start, tb), :]))
            z2 = lax.dot_general(q2s[i], kcat, nt, preferred_element_type=F32)
            for h in range(2):
                z = z2[:, h * tb:(h + 1) * tb]
                soft = jnp.log(1.0 + jnp.exp(-jnp.abs(z)))
                log_beta = jnp.minimum(z, 0.0) - soft
                log_1m = log_beta - z
                if masked:
                    log_1m = jnp.where(below_diag, log_1m, 0.0)
                log_betas.append(log_beta)
                xs.append(log_1m.astype(BF16))
        cs_all = jnp.dot(jnp.concatenate(xs, axis=0), uu, preferred_element_type=F32)
        worst = None
        for i in range(group):
            ws = []
            for h in range(2):
                n = 2 * i + h
                cs = cs_all[n * tb:(n + 1) * tb]
                carry = jnp.where(actives[i], carry_ref[i, h], MASK_VALUE)
                w = jnp.exp(log_betas[n] + cs[:, :tb] + carry)
                if masked:
                    w = jnp.where(below_diag, w, 0.0)
                ws.append(w.astype(BF16))
                carry = carry + cs[:, tb:]
                carry_ref[i, h] = carry
                top = jnp.max(carry)
                worst = top if worst is None else jnp.maximum(worst, top)
            acc_ref[i] += jnp.dot(jnp.concatenate(ws, axis=1), vcats[i], preferred_element_type=F32)
        return worst

    worst0 = process(0, True)
    last_block = step * group + group - 1

    def cond(state):
        d, worst = state
        return jnp.logical_and(d <= last_block, worst >= EXP_ZERO_BELOW)

    def body(state):
        d, _ = state
        return d + 1, process(d, False)

    lax.while_loop(cond, body, (jnp.int32(1), worst0))
    for i in range(group):
        o_ref[i * tb:(i + 1) * tb, :] = acc_ref[i].astype(o_ref.dtype)


def _sb_attention(proj, *, batch, seq, n_heads, q_col, k_col, v_col, tb, group):
    t = proj.shape[0]
    nq = seq // (tb * group)
    pairs = n_heads // 2
    qcb, kcb, vcb = q_col // LANES, k_col // LANES, v_col // LANES
    tri = np.arange(tb)[:, None] > np.arange(tb)[None, :]
    uu = jnp.asarray(np.concatenate([tri, np.ones((tb, tb), bool)], axis=1), BF16)
    return pl.pallas_call(
        functools.partial(_sb_kernel, tb=tb, group=group),
        grid=(batch, pairs, nq),
        in_specs=[
            pl.BlockSpec((tb * group, LANES), lambda b, p, i: (b * nq + i, qcb + p)),
            pl.BlockSpec((seq, LANES), lambda b, p, i: (b, kcb + p)),
            pl.BlockSpec((seq, LANES), lambda b, p, i: (b, vcb + p)),
            pl.BlockSpec((tb, 2 * tb), lambda b, p, i: (0, 0)),
        ],
        out_specs=pl.BlockSpec((tb * group, LANES), lambda b, p, i: (b * nq + i, p)),
        out_shape=jax.ShapeDtypeStruct((t, n_heads * HEAD_DIM), BF16),
        scratch_shapes=[pltpu.VMEM((group, tb, LANES), F32), pltpu.VMEM((group, 2, tb, tb), F32)],
        compiler_params=pltpu.CompilerParams(
            dimension_semantics=("parallel", "parallel", "parallel"),
            vmem_limit_bytes=VMEM_LIMIT_BYTES),
        name="sb_attention",
    )(proj, proj, proj, uu)


def _out_proj_kernel(oa_ref, ob_ref, ga_ref, gb_ref, w_ref, x_ref, o_ref):
    def normed(o_r, g_r):
        o = o_r[...].astype(F32)
        return (o * _rms_scale(o) * g_r[...]).astype(BF16)

    wa = oa_ref.shape[1]
    acc = jnp.dot(normed(oa_ref, ga_ref), w_ref[:wa, :], preferred_element_type=F32)
    acc += jnp.dot(normed(ob_ref, gb_ref), w_ref[wa:, :], preferred_element_type=F32)
    o_ref[...] = x_ref[...] + acc


def _out_proj(o_a, o_b, ga, gb, w, x, *, tm):
    t, d = x.shape
    wa, wb = o_a.shape[1], o_b.shape[1]
    return pl.pallas_call(
        _out_proj_kernel,
        grid=(t // tm,),
        in_specs=[
            pl.BlockSpec((tm, wa), lambda i: (i, 0)),
            pl.BlockSpec((tm, wb), lambda i: (i, 0)),
            pl.BlockSpec((1, wa), lambda i: (0, 0)),
            pl.BlockSpec((1, wb), lambda i: (0, 0)),
            pl.BlockSpec((wa + wb, d), lambda i: (0, 0), pipeline_mode=pl.Buffered(1)),
            pl.BlockSpec((tm, d), lambda i: (i, 0)),
        ],
        out_specs=pl.BlockSpec((tm, d), lambda i: (i, 0)),
        out_shape=jax.ShapeDtypeStruct((t, d), F32),
        compiler_params=pltpu.CompilerParams(
            dimension_semantics=("parallel",),
            vmem_limit_bytes=VMEM_LIMIT_BYTES),
        name="out_proj_residual",
    )(o_a, o_b, ga, gb, w, x)


def _mlp_kernel(x_ref, g_ref, wu_ref, wd_ref, o_ref, h_ref, acc_ref):
    f = pl.program_id(1)

    @pl.when(f == 0)
    def _():
        x = x_ref[...]
        h_ref[...] = (x * _rms_scale(x) * g_ref[...]).astype(h_ref.dtype)
        acc_ref[...] = jnp.zeros_like(acc_ref)

    u = jnp.dot(h_ref[...], wu_ref[...], preferred_element_type=F32)
    u = jnp.maximum(u, 0.0)
    acc_ref[...] += jnp.dot((u * u).astype(BF16), wd_ref[...], preferred_element_type=F32)

    @pl.when(f == pl.num_programs(1) - 1)
    def _():
        o_ref[...] = x_ref[...] + acc_ref[...]


def _mlp(x, g, w_up, w_down, *, tm, tf):
    t, d = x.shape
    ff = w_up.shape[1]
    return pl.pallas_call(
        _mlp_kernel,
        grid=(t // tm, ff // tf),
        in_specs=[
            pl.BlockSpec((tm, d), lambda i, f: (i, 0)),
            pl.BlockSpec((1, d), lambda i, f: (0, 0)),
            pl.BlockSpec((d, tf), lambda i, f: (0, f)),
            pl.BlockSpec((tf, d), lambda i, f: (f, 0)),
        ],
        out_specs=pl.BlockSpec((tm, d), lambda i, f: (i, 0)),
        out_shape=jax.ShapeDtypeStruct((t, d), F32),
        scratch_shapes=[pltpu.VMEM((tm, d), BF16), pltpu.VMEM((tm, d), F32)],
        compiler_params=pltpu.CompilerParams(
            dimension_semantics=("parallel", "arbitrary"),
            vmem_limit_bytes=VMEM_LIMIT_BYTES),
        name="mlp_residual",
    )(x, g, w_up, w_down)


def _tile(total, preferred):
    if total <= preferred:
        return total
    for cand in range(preferred, 0, -LANES):
        if total % cand == 0:
            return cand
    return total


def kernel(x, norm_attn_g, w_in, q_norm_g, k_norm_g, sinks, rel_bias, swa_out_g, sb_out_g,
           w_out, norm_mlp_g, w_up, w_down):
    batch, seq, d_model = x.shape
    depth = w_in.shape[0]
    swa_heads = sinks.shape[1]
    swa_q_w = swa_heads * HEAD_DIM
    sb_w = sb_out_g.shape[1]
    sb_heads = sb_w // HEAD_DIM
    d_in = w_in.shape[2]
    swa_kv_w = (d_in - swa_q_w - 3 * sb_w) // 2
    swa_kv_heads = swa_kv_w // HEAD_DIM
    o1 = swa_q_w
    o2 = o1 + swa_kv_w
    o3 = o2 + swa_kv_w
    o4 = o3 + sb_w
    o5 = o4 + sb_w
    assert seq % BLOCK == 0 and swa_kv_heads % 2 == 0 and sb_heads % 2 == 0
    assert swa_heads % swa_kv_heads == 0 and (swa_heads // swa_kv_heads) % 2 == 0

    t = batch * seq
    tm = _tile(t, 512)
    tn_in = _tile(d_in, 1536)
    tf = _tile(w_up.shape[2], 1024)
    sb_group = next(g for g in (SB_GROUP, 2, 1) if (seq // BLOCK) % g == 0)

    xt = x.reshape(t, d_model).astype(F32)
    swa_group = next(g for g in (SWA_GROUP, 1) if (seq // BLOCK) % g == 0)
    bias = _bias_table(rel_bias, swa_heads // swa_kv_heads)
    row = lambda v: v.reshape(1, -1).astype(F32)

    for l in range(depth):
        proj = _norm_matmul(xt, row(norm_attn_g[l]), w_in[l].astype(BF16), tm=tm, tn=tn_in)
        gq2 = row(jnp.tile(q_norm_g[l], LANES // HEAD_DIM))
        gk2 = row(jnp.tile(k_norm_g[l], LANES // HEAD_DIM))
        o_a = _swa_attention(proj, sinks[l].astype(F32), bias, gq2, gk2, batch=batch, seq=seq,
                             n_heads=swa_heads, n_kv_heads=swa_kv_heads, k_col=o1, v_col=o2,
                             group=swa_group)
        o_b = _sb_attention(proj, batch=batch, seq=seq, n_heads=sb_heads,
                            q_col=o3, k_col=o4, v_col=o5, tb=BLOCK, group=sb_group)
        xt = _out_proj(o_a, o_b, row(swa_out_g[l]), row(sb_out_g[l]), w_out[l].astype(BF16), xt, tm=tm)
        xt = _mlp(xt, row(norm_mlp_g[l]), w_up[l].astype(BF16), w_down[l].astype(BF16), tm=tm, tf=tf)
    return xt.reshape(batch, seq, d_model).astype(x.dtype)
```

```python
import functools
import math

import numpy as np
import jax
import jax.numpy as jnp
from jax import lax
from jax.experimental import pallas as pl
from jax.experimental.pallas import tpu as pltpu

HEAD_DIM = 64
LANES = 128
WINDOW = 128
BLOCK = 128
N_BUCKETS = 32
MAX_DISTANCE = 128
EPS = 1e-6
MASK_VALUE = -1e30
EXP_ZERO_BELOW = -104.0
VMEM_LIMIT_BYTES = 56 * 1024 * 1024
SB_GROUP = 8
SWA_GROUP = 4

F32 = jnp.float32
BF16 = jnp.bfloat16


def _rms_scale(x):
    return lax.rsqrt(jnp.mean(x * x, axis=-1, keepdims=True) + EPS)


def _norm_matmul_kernel(x_ref, g_ref, w_ref, o_ref, *, tn):
    x = x_ref[...]
    h = (x * _rms_scale(x) * g_ref[...]).astype(BF16)
    for c0 in range(0, o_ref.shape[1], tn):
        o_ref[:, c0:c0 + tn] = jnp.dot(h, w_ref[:, c0:c0 + tn],
                                       preferred_element_type=F32).astype(o_ref.dtype)


def _norm_matmul(x, g, w, layer, *, tm, tn):
    t, d = x.shape
    n = w.shape[2]
    return pl.pallas_call(
        functools.partial(_norm_matmul_kernel, tn=tn),
        grid=(t // tm,),
        in_specs=[
            pl.BlockSpec((tm, d), lambda i: (i, 0)),
            pl.BlockSpec((1, d), lambda i: (0, 0)),
            pl.BlockSpec((None, d, n), lambda i: (layer, 0, 0), pipeline_mode=pl.Buffered(1)),
        ],
        out_specs=pl.BlockSpec((tm, n), lambda i: (i, 0)),
        out_shape=jax.ShapeDtypeStruct((t, n), BF16),
        compiler_params=pltpu.CompilerParams(
            dimension_semantics=("parallel",),
            vmem_limit_bytes=VMEM_LIMIT_BYTES),
        name="norm_in_proj",
    )(x, g, w)


def _t5_bucket_np(dist):
    max_exact = N_BUCKETS // 2
    d = np.maximum(dist, 0)
    ratio = np.maximum(d, 1).astype(np.float32) / max_exact
    large = max_exact + (np.log(ratio) / math.log(MAX_DISTANCE / max_exact)
                         * (N_BUCKETS - max_exact)).astype(np.int32)
    large = np.minimum(large, N_BUCKETS - 1)
    return np.where(d < max_exact, d, large).astype(np.int32)


def _bias_table_kernel(rb_ref, bucket_ref, o_ref):
    h = pl.program_id(0)
    bucket = bucket_ref[...]
    acc = jnp.full(bucket.shape, MASK_VALUE, F32)
    for b in range(N_BUCKETS):
        acc = jnp.where(bucket == b, rb_ref[b, h], acc)
    o_ref[0] = acc


def _bias_table(rel_bias, gqa):
    n_heads = rel_bias.shape[1]
    qi = np.arange(BLOCK)[:, None]
    kj = np.arange(2 * BLOCK)[None, :]
    dist = qi + BLOCK - kj
    in_window = (dist >= 0) & (dist < WINDOW)
    bucket = np.where(in_window, _t5_bucket_np(dist), -1).astype(np.int32)
    return pl.pallas_call(
        _bias_table_kernel,
        grid=(n_heads,),
        in_specs=[
            pl.BlockSpec(memory_space=pltpu.SMEM),
            pl.BlockSpec((BLOCK, 2 * BLOCK), lambda h: (0, 0)),
        ],
        out_specs=pl.BlockSpec((1, BLOCK, 2 * BLOCK), lambda h: (h // gqa, (h % gqa) // 2, h % 2)),
        out_shape=jax.ShapeDtypeStruct((n_heads // gqa, (gqa // 2) * BLOCK, 4 * BLOCK), F32),
        name="t5_bias_table",
    )(rel_bias.astype(F32), jnp.asarray(bucket))


def _head_sumsq(xf, ones_blockdiag):
    return jnp.dot((xf * xf).astype(BF16), ones_blockdiag, preferred_element_type=F32)


def _lane_half_copies(xf, lo_half, fill):
    xr = pltpu.roll(xf, HEAD_DIM, axis=1)
    other = jnp.full_like(xf, fill)
    lo = lambda x: jnp.where(lo_half, x, other).astype(BF16)
    hi = lambda x: jnp.where(lo_half, other, x).astype(BF16)
    return [(lo(xf), hi(xr)), (lo(xr), hi(xf))]


def _swa_kernel(sink_ref, q_ref, kg_ref, kp_ref, vg_ref, vp_ref, bias_ref,
                gq_ref, gk_ref, ones_bd_ref, o_ref, *, gqa, group):
    kvp = pl.program_id(1)
    step = pl.program_id(2)
    ppk = gqa // 2
    lo_half = lax.broadcasted_iota(jnp.int32, (1, LANES), 1) < HEAD_DIM
    ones_bd = ones_bd_ref[...]
    nt =(((1,), (1,)), ((), ()))
    scale = 1.0 / math.sqrt(HEAD_DIM)

    kf = jnp.concatenate([kp_ref[...], kg_ref[...]], axis=0).astype(F32)
    kn = kf * lax.rsqrt(_head_sumsq(kf, ones_bd) * (1.0 / HEAD_DIM) + EPS) * gk_ref[...]
    k_copies = _lane_half_copies(kn, lo_half, 0.0)
    v_copies = _lane_half_copies(jnp.concatenate([vp_ref[...], vg_ref[...]], axis=0).astype(F32), lo_half, 1.0)

    col = lax.broadcasted_iota(jnp.int32, (1, 2 * BLOCK), 1)
    row_top = lax.broadcasted_iota(jnp.int32, (ppk * BLOCK, 1), 0)

    for j in range(group):
        r0 = j * BLOCK
        key_ok = jnp.logical_or(col >= BLOCK, step * group + j > 0)
        qn = []
        for r in range(2 * ppk):
            qf = q_ref[r0:r0 + BLOCK, r * LANES:(r + 1) * LANES].astype(F32)
            qn.append((qf * lax.rsqrt(_head_sumsq(qf, ones_bd) * (1.0 / HEAD_DIM) + EPS)
                       * (gq_ref[...] * scale)).astype(BF16))
        for c in range(2):
            k_lo, k_hi = k_copies[c]
            v_lo, v_hi = v_copies[c]
            kcat = jnp.concatenate([k_lo[r0:r0 + 2 * BLOCK], k_hi[r0:r0 + 2 * BLOCK]], axis=0)
            qs = jnp.concatenate(qn[c * ppk:(c + 1) * ppk], axis=0)
            z = lax.dot_general(qs, kcat, nt, preferred_element_type=F32)
            es, sink_terms = [], []
            for h in range(2):
                sl = slice(h * 2 * BLOCK, (h + 1) * 2 * BLOCK)
                s = jnp.where(key_ok, z[:, sl] + bias_ref[c, :, sl], MASK_VALUE)
                head0 = kvp * 2 * gqa + c * gqa + h
                sink = jnp.full((ppk * BLOCK, 1), sink_ref[head0], F32)
                for r in range(1, ppk):
                    sink = jnp.where(row_top >= r * BLOCK, sink_ref[head0 + 2 * r], sink)
                m = jnp.maximum(jnp.max(s, axis=-1, keepdims=True), sink)
                es.append(jnp.exp(s - m).astype(BF16))
                sink_terms.append(jnp.exp(sink - m))
            res = [jnp.dot(es[0], v_lo[r0:r0 + 2 * BLOCK], preferred_element_type=F32),
                   jnp.dot(es[1], v_hi[r0:r0 + 2 * BLOCK], preferred_element_type=F32)]
            num = jnp.where(lo_half, res[0], res[1])
            den = jnp.where(lo_half, pltpu.roll(res[0], HEAD_DIM, axis=1) + sink_terms[0],
                            pltpu.roll(res[1], HEAD_DIM, axis=1) + sink_terms[1])
            out = (num * (1.0 / den)).astype(o_ref.dtype)
            for r in range(ppk):
                lane0 = (c * ppk + r) * LANES
                o_ref[r0:r0 + BLOCK, lane0:lane0 + LANES] = out[r * BLOCK:(r + 1) * BLOCK]


def _swa_attention(proj, sinks, bias, gq2, gk2, *, batch, seq, n_heads, n_kv_heads, k_col, v_col, group):
    t = proj.shape[0]
    nb = seq // BLOCK
    ns = nb // group
    kv_pairs = n_kv_heads // 2
    gqa = n_heads // n_kv_heads
    qw = 2 * gqa * HEAD_DIM
    kcb = k_col // LANES
    vcb = v_col // LANES
    ones_bd = jnp.asarray(np.kron(np.eye(2), np.ones((HEAD_DIM, HEAD_DIM))), BF16)

    cur = lambda cb: (lambda b, g, i: (b * ns + i, cb + g))
    prev = lambda cb: (lambda b, g, i: (b * nb + jnp.maximum(i * group - 1, 0), cb + g))
    return pl.pallas_call(
        functools.partial(_swa_kernel, gqa=gqa, group=group),
        grid=(batch, kv_pairs, ns),
        in_specs=[
            pl.BlockSpec(memory_space=pltpu.SMEM),
            pl.BlockSpec((group * BLOCK, qw), lambda b, g, i: (b * ns + i, g)),
            pl.BlockSpec((group * BLOCK, LANES), cur(kcb)),
            pl.BlockSpec((BLOCK, LANES), prev(kcb)),
            pl.BlockSpec((group * BLOCK, LANES), cur(vcb)),
            pl.BlockSpec((BLOCK, LANES), prev(vcb)),
            pl.BlockSpec((2, (gqa // 2) * BLOCK, 4 * BLOCK), lambda b, g, i: (g, 0, 0)),
            pl.BlockSpec((1, LANES), lambda b, g, i: (0, 0)),
            pl.BlockSpec((1, LANES), lambda b, g, i: (0, 0)),
            pl.BlockSpec((LANES, LANES), lambda b, g, i: (0, 0)),
        ],
        out_specs=pl.BlockSpec((group * BLOCK, qw), lambda b, g, i: (b * ns + i, g)),
        out_shape=jax.ShapeDtypeStruct((t, n_heads * HEAD_DIM), BF16),
        compiler_params=pltpu.CompilerParams(
            dimension_semantics=("parallel", "parallel", "parallel"),
            vmem_limit_bytes=VMEM_LIMIT_BYTES),
        name="swa_attention",
    )(sinks, proj, proj, proj, proj, proj, bias, gq2, gk2, ones_bd)


def _sb_kernel(q_ref, k_ref, v_ref, u_ref, o_ref, acc_ref, carry_ref, *, tb, group):
    step = pl.program_id(2)
    lo_half = lax.broadcasted_iota(jnp.int32, (1, LANES), 1) < HEAD_DIM
    scale = 1.0 / math.sqrt(HEAD_DIM)
    uu = u_ref[...]
    row = lax.broadcasted_iota(jnp.int32, (tb, tb), 0)
    colk = lax.broadcasted_iota(jnp.int32, (tb, tb), 1)
    below_diag = colk < row
    nt = (((1,), (1,)), ((), ()))

    acc_ref[...] = jnp.zeros_like(acc_ref)
    carry_ref[...] = jnp.zeros_like(carry_ref)

    def split_heads(x2):
        zero = jnp.zeros_like(x2)
        return jnp.concatenate([jnp.where(lo_half, x2, zero), jnp.where(lo_half, zero, x2)], axis=0)

    q_heads = [split_heads((q_ref[i * tb:(i + 1) * tb, :].astype(F32) * scale).astype(BF16))
               for i in range(group)]

    def process(d, masked):
        actives, vcats, log_betas, xs = [], [], [], []
        for i in range(group):
            kb = step * group + i - d
            actives.append(kb >= 0)
            start = pl.multiple_of(jnp.maximum(kb, 0) * tb, tb)
            vcats.append(split_heads(v_ref[pl.ds(start, tb), :]))
            z2 = lax.dot_general(q_heads[i], k_ref[pl.ds(start, tb), :], nt,
                                 preferred_element_type=F32)
            for h in range(2):
                z = z2[h * tb:(h + 1) * tb, :]
                soft = jnp.log(1.0 + jnp.exp(-jnp.abs(z)))
                log_beta = jnp.minimum(z, 0.0) - soft
                log_1m = log_beta - z
                if masked:
                    log_1m = jnp.where(below_diag, log_1m, 0.0)
                log_betas.append(log_beta)
                xs.append(log_1m.astype(BF16))
        cs_all = jnp.dot(jnp.concatenate(xs, axis=0), uu, preferred_element_type=F32)
        worst = None
        for i in range(group):
            ws = []
            for h in range(2):
                n = 2 * i + h
                cs = cs_all[n * tb:(n + 1) * tb]
                carry = jnp.where(actives[i], carry_ref[i, h], MASK_VALUE)
                w = jnp.exp(log_betas[n] + cs[:, :tb] + carry)
                if masked:
                    w = jnp.where(below_diag, w, 0.0)
                ws.append(w.astype(BF16))
                carry = carry + cs[:, tb:]
                carry_ref[i, h] = carry
                top = jnp.max(carry)
                worst = top if worst is None else jnp.maximum(worst, top)
            acc_ref[i] += jnp.dot(jnp.concatenate(ws, axis=1), vcats[i], preferred_element_type=F32)
        return worst

    worst0 = process(0, True)
    last_block = step * group + group - 1

    def cond(state):
        d, worst = state
        return jnp.logical_and(d <= last_block, worst >= EXP_ZERO_BELOW)

    def body(state):
        d, _ = state
        return d + 1, process(d, False)

    lax.while_loop(cond, body, (jnp.int32(1), worst0))
    for i in range(group):
        o_ref[i * tb:(i + 1) * tb, :] = acc_ref[i].astype(o_ref.dtype)


def _sb_attention(proj, *, batch, seq, n_heads, q_col, k_col, v_col, tb, group):
    t = proj.shape[0]
    nq = seq // (tb * group)
    pairs = n_heads // 2
    qcb, kcb, vcb = q_col // LANES, k_col // LANES, v_col // LANES
    tri = np.arange(tb)[:, None] > np.arange(tb)[None, :]
    uu = jnp.asarray(np.concatenate([tri, np.ones((tb, tb), bool)], axis=1), BF16)
    return pl.pallas_call(
        functools.partial(_sb_kernel, tb=tb, group=group),
        grid=(batch, pairs, nq),
        in_specs=[
            pl.BlockSpec((tb * group, LANES), lambda b, p, i: (b * nq + i, qcb + p)),
            pl.BlockSpec((seq, LANES), lambda b, p, i: (b, kcb + p)),
            pl.BlockSpec((seq, LANES), lambda b, p, i: (b, vcb + p)),
            pl.BlockSpec((tb, 2 * tb), lambda b, p, i: (0, 0)),
        ],
        out_specs=pl.BlockSpec((tb * group, LANES), lambda b, p, i: (b * nq + i, p)),
        out_shape=jax.ShapeDtypeStruct((t, n_heads * HEAD_DIM), BF16),
        scratch_shapes=[pltpu.VMEM((group, tb, LANES), F32), pltpu.VMEM((group, 2, tb, tb), F32)],
        compiler_params=pltpu.CompilerParams(
            dimension_semantics=("parallel", "parallel", "parallel"),
            vmem_limit_bytes=VMEM_LIMIT_BYTES),
        name="sb_attention",
    )(proj, proj, proj, uu)


def _out_proj_kernel(oa_ref, ob_ref, ga_ref, gb_ref, w_ref, x_ref, o_ref):
    def normed(o_r, g_r):
        o = o_r[...].astype(F32)
        return (o * _rms_scale(o) * g_r[...]).astype(BF16)

    wa = oa_ref.shape[1]
    acc = jnp.dot(normed(oa_ref, ga_ref), w_ref[:wa, :], preferred_element_type=F32)
    acc += jnp.dot(normed(ob_ref, gb_ref), w_ref[wa:, :], preferred_element_type=F32)
    o_ref[...] = x_ref[...] + acc


def _out_proj(o_a, o_b, ga, gb, w, layer, x, *, tm):
    t, d = x.shape
    wa, wb = o_a.shape[1], o_b.shape[1]
    return pl.pallas_call(
        _out_proj_kernel,
        grid=(t // tm,),
        in_specs=[
            pl.BlockSpec((tm, wa), lambda i: (i, 0)),
            pl.BlockSpec((tm, wb), lambda i: (i, 0)),
            pl.BlockSpec((1, wa), lambda i: (0, 0)),
            pl.BlockSpec((1, wb), lambda i: (0, 0)),
            pl.BlockSpec((None, wa + wb, d), lambda i: (layer, 0, 0), pipeline_mode=pl.Buffered(1)),
            pl.BlockSpec((tm, d), lambda i: (i, 0)),
        ],
        out_specs=pl.BlockSpec((tm, d), lambda i: (i, 0)),
        out_shape=jax.ShapeDtypeStruct((t, d), F32),
        compiler_params=pltpu.CompilerParams(
            dimension_semantics=("parallel",),
            vmem_limit_bytes=VMEM_LIMIT_BYTES),
        name="out_proj_residual",
    )(o_a, o_b, ga, gb, w, x)


def _mlp_kernel(x_ref, g_ref, wu_ref, wd_ref, o_ref, h_ref, acc_ref):
    f = pl.program_id(1)

    @pl.when(f == 0)
    def _():
        x = x_ref[...]
        h_ref[...] = (x * _rms_scale(x) * g_ref[...]).astype(h_ref.dtype)
        acc_ref[...] = jnp.zeros_like(acc_ref)

    u = jnp.dot(h_ref[...], wu_ref[...], preferred_element_type=F32)
    u = jnp.maximum(u, 0.0)
    acc_ref[...] += jnp.dot((u * u).astype(BF16), wd_ref[...], preferred_element_type=F32)

    @pl.when(f == pl.num_programs(1) - 1)
    def _():
        o_ref[...] = x_ref[...] + acc_ref[...]


def _mlp(x, g, w_up, w_down, layer, *, tm, tf):
    t, d = x.shape
    ff = w_up.shape[2]
    return pl.pallas_call(
        _mlp_kernel,
        grid=(t // tm, ff // tf),
        in_specs=[
            pl.BlockSpec((tm, d), lambda i, f: (i, 0)),
            pl.BlockSpec((1, d), lambda i, f: (0, 0)),
            pl.BlockSpec((None, d, tf), lambda i, f: (layer, 0, f)),
            pl.BlockSpec((None, tf, d), lambda i, f: (layer, f, 0)),
        ],
        out_specs=pl.BlockSpec((tm, d), lambda i, f: (i, 0)),
        out_shape=jax.ShapeDtypeStruct((t, d), F32),
        scratch_shapes=[pltpu.VMEM((tm, d), BF16), pltpu.VMEM((tm, d), F32)],
        compiler_params=pltpu.CompilerParams(
            dimension_semantics=("parallel", "arbitrary"),
            vmem_limit_bytes=VMEM_LIMIT_BYTES),
        name="mlp_residual",
    )(x, g, w_up, w_down)


def _tile(total, preferred):
    if total <= preferred:
        return total
    for cand in range(preferred, 0, -LANES):
        if total % cand == 0:
            return cand
    return total


def kernel(x, norm_attn_g, w_in, q_norm_g, k_norm_g, sinks, rel_bias, swa_out_g, sb_out_g,
           w_out, norm_mlp_g, w_up, w_down):
    batch, seq, d_model = x.shape
    depth = w_in.shape[0]
    swa_heads = sinks.shape[1]
    swa_q_w = swa_heads * HEAD_DIM
    sb_w = sb_out_g.shape[1]
    sb_heads = sb_w // HEAD_DIM
    d_in = w_in.shape[2]
    swa_kv_w = (d_in - swa_q_w - 3 * sb_w) // 2
    swa_kv_heads = swa_kv_w // HEAD_DIM
    o1 = swa_q_w
    o2 = o1 + swa_kv_w
    o3 = o2 + swa_kv_w
    o4 = o3 + sb_w
    o5 = o4 + sb_w
    assert seq % BLOCK == 0 and swa_kv_heads % 2 == 0 and sb_heads % 2 == 0
    assert swa_heads % swa_kv_heads == 0 and (swa_heads // swa_kv_heads) % 2 == 0

    t = batch * seq
    tm = _tile(t, 512)
    tn_in = _tile(d_in, 1536)
    tf = _tile(w_up.shape[2], 1024)
    sb_group = next(g for g in (SB_GROUP, 2, 1) if (seq // BLOCK) % g == 0)

    xt = x.reshape(t, d_model).astype(F32)
    swa_group = next(g for g in (SWA_GROUP, 1) if (seq // BLOCK) % g == 0)
    bias = _bias_table(rel_bias, swa_heads // swa_kv_heads)
    row = lambda v: v.reshape(1, -1).astype(F32)

    w_in, w_out, w_up, w_down = (w.astype(BF16) for w in (w_in, w_out, w_up, w_down))
    for l in range(depth):
        proj = _norm_matmul(xt, row(norm_attn_g[l]), w_in, l, tm=tm, tn=tn_in)
        gq2 = row(jnp.tile(q_norm_g[l], LANES // HEAD_DIM))
        gk2 = row(jnp.tile(k_norm_g[l], LANES // HEAD_DIM))
        o_a = _swa_attention(proj, sinks[l].astype(F32), bias, gq2, gk2, batch=batch, seq=seq,
                             n_heads=swa_heads, n_kv_heads=swa_kv_heads, k_col=o1, v_col=o2,
                             group=swa_group)
        o_b = _sb_attention(proj, batch=batch, seq=seq, n_heads=sb_heads,
                            q_col=o3, k_col=o4, v_col=o5, tb=BLOCK, group=sb_group)
        xt = _out_proj(o_a, o_b, row(swa_out_g[l]), row(sb_out_g[l]), w_out, l, xt, tm=tm)
        xt = _mlp(xt, row(norm_mlp_g[l]), w_up, w_down, l, tm=tm, tf=tf)
    return xt.reshape(batch, seq, d_model).astype(x.dtype)
```

```python
import functools
import math

import numpy as np
import jax
import jax.numpy as jnp
from jax import lax
from jax.experimental import pallas as pl
from jax.experimental.pallas import tpu as pltpu

HEAD_DIM = 64
LANES = 128
BF16_SUBLANES = 16
WINDOW = 128
BLOCK = 128
N_BUCKETS = 32
MAX_DISTANCE = 128
EPS = 1e-6
MASK_VALUE = -1e30
EXP_ZERO_BELOW = -104.0
VMEM_LIMIT_BYTES = 56 * 1024 * 1024
SB_GROUP = 8
SWA_GROUP = 4
SWA_CHUNK = 16

F32 = jnp.float32
BF16 = jnp.bfloat16


def _rms_scale(x):
    return lax.rsqrt(jnp.mean(x * x, axis=-1, keepdims=True) + EPS)


def _norm_matmul_kernel(x_ref, g_ref, w_ref, o_ref, *, tn):
    x = x_ref[...]
    h = (x * _rms_scale(x) * g_ref[...]).astype(BF16)
    for c0 in range(0, o_ref.shape[1], tn):
        o_ref[:, c0:c0 + tn] = jnp.dot(h, w_ref[:, c0:c0 + tn],
                                       preferred_element_type=F32).astype(o_ref.dtype)


def _norm_matmul(x, g, w, *, tm, tn):
    t, d = x.shape
    n = w.shape[1]
    return pl.pallas_call(
        functools.partial(_norm_matmul_kernel, tn=tn),
        grid=(t // tm,),
        in_specs=[
            pl.BlockSpec((tm, d), lambda i: (i, 0)),
            pl.BlockSpec((1, d), lambda i: (0, 0)),
            pl.BlockSpec((d, n), lambda i: (0, 0), pipeline_mode=pl.Buffered(1)),
        ],
        out_specs=pl.BlockSpec((tm, n), lambda i: (i, 0)),
        out_shape=jax.ShapeDtypeStruct((t, n), BF16),
        compiler_params=pltpu.CompilerParams(
            dimension_semantics=("parallel",),
            vmem_limit_bytes=VMEM_LIMIT_BYTES),
        name="norm_in_proj",
    )(x, g, w)


def _t5_bucket_np(dist):
    max_exact = N_BUCKETS // 2
    d = np.maximum(dist, 0)
    ratio = np.maximum(d, 1).astype(np.float32) / max_exact
    large = max_exact + (np.log(ratio) / math.log(MAX_DISTANCE / max_exact)
                         * (N_BUCKETS - max_exact)).astype(np.int32)
    large = np.minimum(large, N_BUCKETS - 1)
    return np.where(d < max_exact, d, large).astype(np.int32)


def _bias_table_kernel(rb_ref, bucket_ref, o_ref):
    h = pl.program_id(0)
    bucket = bucket_ref[...]
    acc = jnp.full(bucket.shape, MASK_VALUE, F32)
    for b in range(N_BUCKETS):
        acc = jnp.where(bucket == b, rb_ref[b, h], acc)
    o_ref[0] = acc


def _bias_table(rel_bias, gqa):
    n_heads = rel_bias.shape[1]
    qi = np.arange(BLOCK)[:, None]
    kj = np.arange(2 * BLOCK)[None, :]
    dist = qi + BLOCK - kj
    in_window = (dist >= 0) & (dist < WINDOW)
    bucket = np.where(in_window, _t5_bucket_np(dist), -1).astype(np.int32)
    return pl.pallas_call(
        _bias_table_kernel,
        grid=(n_heads,),
        in_specs=[
            pl.BlockSpec(memory_space=pltpu.SMEM),
            pl.BlockSpec((BLOCK, 2 * BLOCK), lambda h: (0, 0)),
        ],
        out_specs=pl.BlockSpec((1, BLOCK, 2 * BLOCK), lambda h: (h // gqa, (h % gqa) // 2, h % 2)),
        out_shape=jax.ShapeDtypeStruct((n_heads // gqa, (gqa // 2) * BLOCK, 4 * BLOCK), F32),
        name="t5_bias_table",
    )(rel_bias.astype(F32), jnp.asarray(bucket))


def _head_sumsq(xf, ones_blockdiag):
    return jnp.dot((xf * xf).astype(BF16), ones_blockdiag, preferred_element_type=F32)


def _lane_half_copies(xf, lo_half, fill):
    xr = pltpu.roll(xf, HEAD_DIM, axis=1)
    other = jnp.full_like(xf, fill)
    lo = lambda x: jnp.where(lo_half, x, other).astype(BF16)
    hi = lambda x: jnp.where(lo_half, other, x).astype(BF16)
    return [(lo(xf), hi(xr)), (lo(xr), hi(xf))]


def _swa_kernel(sink_ref, q_ref, kg_ref, kp_ref, vg_ref, vp_ref, bias_ref,
                gq_ref, gk_ref, ones_bd_ref, o_ref, *, gqa, group):
    n_inner = q_ref.shape[0] // (group * BLOCK)
    lax.fori_loop(0, n_inner, lambda s, _: _swa_group(
        s, pl.program_id(2) * n_inner + s, sink_ref, q_ref, kg_ref, kp_ref, vg_ref, vp_ref, bias_ref,
        gq_ref, gk_ref, ones_bd_ref, o_ref, gqa=gqa, group=group), 0)


def _swa_group(s, step, sink_ref, q_ref, kg_ref, kp_ref, vg_ref, vp_ref, bias_ref,
               gq_ref, gk_ref, ones_bd_ref, o_ref, *, gqa, group):
    kvp = pl.program_id(1)
    ppk = gqa // 2
    lo_half = lax.broadcasted_iota(jnp.int32, (1, LANES), 1) < HEAD_DIM
    ones_bd = ones_bd_ref[...]
    nt = (((1,), (1,)), ((), ()))
    scale = 1.0 / math.sqrt(HEAD_DIM)
    base = pl.multiple_of(s * (group * BLOCK), BLOCK)
    before = pl.multiple_of(jnp.maximum(base - BLOCK, 0), BLOCK)

    def with_previous_block(group_ref, prev_ref):
        prev = jnp.where(s == 0, prev_ref[...], group_ref[pl.ds(before, BLOCK), :])
        return jnp.concatenate([prev, group_ref[pl.ds(base, group * BLOCK), :]], axis=0).astype(F32)

    kf = with_previous_block(kg_ref, kp_ref)
    kn = kf * lax.rsqrt(_head_sumsq(kf, ones_bd) * (1.0 / HEAD_DIM) + EPS) * gk_ref[...]
    k_copies = _lane_half_copies(kn, lo_half, 0.0)
    v_copies = _lane_half_copies(with_previous_block(vg_ref, vp_ref), lo_half, 1.0)

    col = lax.broadcasted_iota(jnp.int32, (1, 2 * BLOCK), 1)
    row_top = lax.broadcasted_iota(jnp.int32, (ppk * BLOCK, 1), 0)

    for j in range(group):
        r0 = j * BLOCK
        rows = pl.ds(pl.multiple_of(base + r0, BLOCK), BLOCK)
        key_ok = jnp.logical_or(col >= BLOCK, step * group + j > 0)
        qn = []
        for r in range(2 * ppk):
            qf = q_ref[rows, r * LANES:(r + 1) * LANES].astype(F32)
            qn.append((qf * lax.rsqrt(_head_sumsq(qf, ones_bd) * (1.0 / HEAD_DIM) + EPS)
                       * (gq_ref[...] * scale)).astype(BF16))
        for c in range(2):
            k_lo, k_hi = k_copies[c]
            v_lo, v_hi = v_copies[c]
            kcat = jnp.concatenate([k_lo[r0:r0 + 2 * BLOCK], k_hi[r0:r0 + 2 * BLOCK]], axis=0)
            qs = jnp.concatenate(qn[c * ppk:(c + 1) * ppk], axis=0)
            z = lax.dot_general(qs, kcat, nt, preferred_element_type=F32)
            es, sink_terms = [], []
            for h in range(2):
                sl = slice(h * 2 * BLOCK, (h + 1) * 2 * BLOCK)
                s = jnp.where(key_ok, z[:, sl] + bias_ref[c, :, sl], MASK_VALUE)
                head0 = kvp * 2 * gqa + c * gqa + h
                sink = jnp.full((ppk * BLOCK, 1), sink_ref[head0], F32)
                for r in range(1, ppk):
                    sink = jnp.where(row_top >= r * BLOCK, sink_ref[head0 + 2 * r], sink)
                m = jnp.maximum(jnp.max(s, axis=-1, keepdims=True), sink)
                es.append(jnp.exp(s - m).astype(BF16))
                sink_terms.append(jnp.exp(sink - m))
            res = [jnp.dot(es[0], v_lo[r0:r0 + 2 * BLOCK], preferred_element_type=F32),
                   jnp.dot(es[1], v_hi[r0:r0 + 2 * BLOCK], preferred_element_type=F32)]
            num = jnp.where(lo_half, res[0], res[1])
            den = jnp.where(lo_half, pltpu.roll(res[0], HEAD_DIM, axis=1) + sink_terms[0],
                            pltpu.roll(res[1], HEAD_DIM, axis=1) + sink_terms[1])
            out = (num * (1.0 / den)).astype(o_ref.dtype)
            for r in range(ppk):
                lane0 = (c * ppk + r) * LANES
                o_ref[rows, lane0:lane0 + LANES] = out[r * BLOCK:(r + 1) * BLOCK]
    return 0


def _swa_attention(proj, sinks, bias, gq2, gk2, *, batch, seq, n_heads, n_kv_heads, k_col, v_col, group, chunk):
    t = proj.shape[0]
    nb = seq // BLOCK
    ns = nb // chunk
    kv_pairs = n_kv_heads // 2
    gqa = n_heads // n_kv_heads
    qw = 2 * gqa * HEAD_DIM
    kcb = k_col // LANES
    vcb = v_col // LANES
    ones_bd = jnp.asarray(np.kron(np.eye(2), np.ones((HEAD_DIM, HEAD_DIM))), BF16)

    cur = lambda cb: (lambda b, g, i: (b * ns + i, cb + g))
    prev = lambda cb: (lambda b, g, i: (b * nb + jnp.maximum(i * chunk - 1, 0), cb + g))
    return pl.pallas_call(
        functools.partial(_swa_kernel, gqa=gqa, group=group),
        grid=(batch, kv_pairs, ns),
        in_specs=[
            pl.BlockSpec(memory_space=pltpu.SMEM),
            pl.BlockSpec((chunk * BLOCK, qw), lambda b, g, i: (b * ns + i, g)),
            pl.BlockSpec((chunk * BLOCK, LANES), cur(kcb)),
            pl.BlockSpec((BLOCK, LANES), prev(kcb)),
            pl.BlockSpec((chunk * BLOCK, LANES), cur(vcb)),
            pl.BlockSpec((BLOCK, LANES), prev(vcb)),
            pl.BlockSpec((2, (gqa // 2) * BLOCK, 4 * BLOCK), lambda b, g, i: (g, 0, 0)),
            pl.BlockSpec((1, LANES), lambda b, g, i: (0, 0)),
            pl.BlockSpec((1, LANES), lambda b, g, i: (0, 0)),
            pl.BlockSpec((LANES, LANES), lambda b, g, i: (0, 0)),
        ],
        out_specs=pl.BlockSpec((chunk * BLOCK, qw), lambda b, g, i: (b * ns + i, g)),
        out_shape=jax.ShapeDtypeStruct((t, n_heads * HEAD_DIM), BF16),
        compiler_params=pltpu.CompilerParams(
            dimension_semantics=("parallel", "parallel", "parallel"),
            vmem_limit_bytes=VMEM_LIMIT_BYTES),
        name="swa_attention",
    )(sinks, proj, proj, proj, proj, proj, bias, gq2, gk2, ones_bd)


def _sb_kernel(q_ref, k_ref, v_ref, u_ref, o_ref, acc_ref, carry_ref, z_ref, w_ref, *, tb, group):
    n_groups = q_ref.shape[0] // (group * tb)
    acc_ref[...] = jnp.zeros_like(acc_ref)
    w_ref[...] = jnp.zeros_like(w_ref)

    def body(step, d_prev):
        _sb_finish_group(jnp.maximum(step - 1, 0), d_prev, v_ref, o_ref, acc_ref, w_ref, tb=tb, group=group)
        return _sb_walk_group(step, q_ref, k_ref, v_ref, u_ref, acc_ref, carry_ref, z_ref, w_ref,
                              tb=tb, group=group)

    d_last = lax.fori_loop(0, n_groups, body, jnp.int32(1))
    _sb_finish_group(n_groups - 1, d_last, v_ref, o_ref, acc_ref, w_ref, tb=tb, group=group)


def _split_heads(x2):
    lo_half = lax.broadcasted_iota(jnp.int32, (1, LANES), 1) < HEAD_DIM
    zero = jnp.zeros_like(x2)
    return jnp.concatenate([jnp.where(lo_half, x2, zero), jnp.where(lo_half, zero, x2)], axis=0)


def _sb_finish_group(step, d_end, v_ref, o_ref, acc_ref, w_ref, *, tb, group):
    d = d_end - 1
    for i in range(group):
        start = pl.multiple_of(jnp.maximum(step * group + i - d, 0) * tb, tb)
        acc = acc_ref[i] + jnp.dot(w_ref[d & 1, i], _split_heads(v_ref[pl.ds(start, tb), :]),
                                   preferred_element_type=F32)
        o_ref[pl.ds(pl.multiple_of((step * group + i) * tb, tb), tb), :] = acc.astype(o_ref.dtype)


def _sb_walk_group(step, q_ref, k_ref, v_ref, u_ref, acc_ref, carry_ref, z_ref, w_ref, *, tb, group):
    base = step * (group * tb)
    scale = 1.0 / math.sqrt(HEAD_DIM)
    uu = u_ref[...]
    row = lax.broadcasted_iota(jnp.int32, (tb, tb), 0)
    colk = lax.broadcasted_iota(jnp.int32, (tb, tb), 1)
    below_diag = colk < row
    nt = (((1,), (1,)), ((), ()))
    split_heads = _split_heads

    acc_ref[...] = jnp.zeros_like(acc_ref)
    carry_ref[...] = jnp.zeros_like(carry_ref)

    q_heads = [split_heads((q_ref[pl.ds(pl.multiple_of(base + i * tb, tb), tb), :].astype(F32)
                            * scale).astype(BF16)) for i in range(group)]

    def tile_start(i, d):
        kb = step * group + i - d
        return pl.multiple_of(jnp.maximum(kb, 0) * tb, tb), kb >= 0

    def issue_scores(d):
        for i in range(group):
            start, _ = tile_start(i, d)
            z_ref[d & 1, i] = lax.dot_general(q_heads[i], k_ref[pl.ds(start, tb), :], nt,
                                              preferred_element_type=F32)

    def weights(d, masked):
        log_betas, xs = [], []
        for i in range(group):
            for h in range(2):
                z = z_ref[d & 1, i, h * tb:(h + 1) * tb, :]
                neg_abs = lax.bitcast_convert_type(
                    lax.bitcast_convert_type(z, jnp.int32) | jnp.int32(-2 ** 31), F32)
                soft = jnp.log(1.0 + jnp.exp(neg_abs))
                log_beta = jnp.minimum(z, 0.0) - soft
                log_1m = log_beta - z
                if masked:
                    log_1m = jnp.where(below_diag, log_1m, 0.0)
                log_betas.append(log_beta)
                xs.append(log_1m.astype(BF16))
        cs_all = jnp.dot(jnp.concatenate(xs, axis=0), uu, preferred_element_type=F32)
        worst = None
        for i in range(group):
            _, active = tile_start(i, d)
            for h in range(2):
                n = 2 * i + h
                cs = cs_all[n * tb:(n + 1) * tb]
                carry = jnp.where(active, carry_ref[i, h], MASK_VALUE)
                w = jnp.exp(log_betas[n] + cs[:, :tb] + carry)
                if masked:
                    w = jnp.where(below_diag, w, 0.0)
                w_ref[d & 1, i, :, h * tb:(h + 1) * tb] = w.astype(BF16)
                carry = carry + cs[:, tb:]
                carry_ref[i, h] = carry
                top = jnp.max(carry)
                worst = top if worst is None else jnp.maximum(worst, top)
        return worst

    def accumulate(d):
        for i in range(group):
            start, _ = tile_start(i, d)
            acc_ref[i] += jnp.dot(w_ref[d & 1, i], split_heads(v_ref[pl.ds(start, tb), :]),
                                  preferred_element_type=F32)

    zero = jnp.int32(0)
    issue_scores(zero)
    worst0 = weights(zero, True)
    issue_scores(zero + 1)
    last_block = step * group + group - 1

    def cond(state):
        d, worst = state
        return jnp.logical_and(d <= last_block, worst >= EXP_ZERO_BELOW)

    def body(state):
        d, _ = state
        accumulate(d - 1)
        worst = weights(d, False)
        issue_scores(d + 1)
        return d + 1, worst

    d_end, _ = lax.while_loop(cond, body, (zero + 1, worst0))
    return d_end


def _sb_attention(proj, *, batch, seq, n_heads, q_col, k_col, v_col, tb, group):
    t = proj.shape[0]
    pairs = n_heads // 2
    qcb, kcb, vcb = q_col // LANES, k_col // LANES, v_col // LANES
    tri = np.arange(tb)[:, None] > np.arange(tb)[None, :]
    uu = jnp.asarray(np.concatenate([tri, np.ones((tb, tb), bool)], axis=1), BF16)
    return pl.pallas_call(
        functools.partial(_sb_kernel, tb=tb, group=group),
        grid=(batch, pairs),
        in_specs=[
            pl.BlockSpec((seq, LANES), lambda b, p: (b, qcb + p)),
            pl.BlockSpec((seq, LANES), lambda b, p: (b, kcb + p)),
            pl.BlockSpec((seq, LANES), lambda b, p: (b, vcb + p)),
            pl.BlockSpec((tb, 2 * tb), lambda b, p: (0, 0)),
        ],
        out_specs=pl.BlockSpec((seq, LANES), lambda b, p: (b, p)),
        out_shape=jax.ShapeDtypeStruct((t, n_heads * HEAD_DIM), BF16),
        scratch_shapes=[pltpu.VMEM((group, tb, LANES), F32),
                        pltpu.VMEM((group, 2, tb, tb), F32),
                        pltpu.VMEM((2, group, 2 * tb, tb), F32),
                        pltpu.VMEM((2, group, tb, 2 * tb), BF16)],
        compiler_params=pltpu.CompilerParams(
            dimension_semantics=("parallel", "parallel"),
            vmem_limit_bytes=VMEM_LIMIT_BYTES),
        name="sb_attention",
    )(proj, proj, proj, uu)


def _out_proj_kernel(oa_ref, ob_ref, ga_ref, gb_ref, w_ref, x_ref, o_ref):
    def normed(o_r, g_r):
        o = o_r[...].astype(F32)
        return (o * _rms_scale(o) * g_r[...]).astype(BF16)

    wa = oa_ref.shape[1]
    acc = jnp.dot(normed(oa_ref, ga_ref), w_ref[:wa, :], preferred_element_type=F32)
    acc += jnp.dot(normed(ob_ref, gb_ref), w_ref[wa:, :], preferred_element_type=F32)
    o_ref[...] = x_ref[...] + acc


def _out_proj(o_a, o_b, ga, gb, w, x, *, tm):
    t, d = x.shape
    wa, wb = o_a.shape[1], o_b.shape[1]
    return pl.pallas_call(
        _out_proj_kernel,
        grid=(t // tm,),
        in_specs=[
            pl.BlockSpec((tm, wa), lambda i: (i, 0)),
            pl.BlockSpec((tm, wb), lambda i: (i, 0)),
            pl.BlockSpec((1, wa), lambda i: (0, 0)),
            pl.BlockSpec((1, wb), lambda i: (0, 0)),
            pl.BlockSpec((wa + wb, d), lambda i: (0, 0), pipeline_mode=pl.Buffered(1)),
            pl.BlockSpec((tm, d), lambda i: (i, 0)),
        ],
        out_specs=pl.BlockSpec((tm, d), lambda i: (i, 0)),
        out_shape=jax.ShapeDtypeStruct((t, d), F32),
        compiler_params=pltpu.CompilerParams(
            dimension_semantics=("parallel",),
            vmem_limit_bytes=VMEM_LIMIT_BYTES),
        name="out_proj_residual",
    )(o_a, o_b, ga, gb, w, x)


def _mlp_kernel(x_ref, g_ref, wu_ref, wd_ref, *refs, n_cast):
    cast_src, o_ref, cast_dst = refs[:n_cast], refs[n_cast], refs[n_cast + 1:2 * n_cast + 1]
    h_ref, acc_ref = refs[2 * n_cast + 1:]
    f = pl.program_id(1)
    last = pl.num_programs(1) - 1

    def ffn_slice(h):
        for src, dst in zip(cast_src, cast_dst):
            dst[...] = src[...].astype(dst.dtype)
        u = jnp.maximum(jnp.dot(h, wu_ref[...], preferred_element_type=F32), 0.0)
        return jnp.dot((u * u).astype(BF16), wd_ref[...], preferred_element_type=F32)

    @pl.when(f == 0)
    def _():
        x = x_ref[...]
        h = (x * _rms_scale(x) * g_ref[...]).astype(h_ref.dtype)
        h_ref[...] = h
        acc_ref[...] = ffn_slice(h)

    @pl.when(jnp.logical_and(f > 0, f < last))
    def _():
        acc_ref[...] += ffn_slice(h_ref[...])

    @pl.when(f == last)
    def _():
        o_ref[...] = x_ref[...] + acc_ref[...] + ffn_slice(h_ref[...])


def _chunking(rows, cols, n_chunks):
    for col_chunks in (1, 2, 4, 8, 16):
        row_chunks, rem = divmod(n_chunks, col_chunks)
        if rem or rows % row_chunks or cols % col_chunks:
            continue
        br, bc = rows // row_chunks, cols // col_chunks
        if br % BF16_SUBLANES == 0 and bc % LANES == 0:
            return br, bc, col_chunks
    return None


def _mlp(x, g, w_up, w_down, cast_weights=(), cast_layer=0, *, tm, tf):
    t, d = x.shape
    ff = w_up.shape[1]
    nf = ff // tf
    assert nf >= 2, "first and last d_ff slices are distinct code paths"
    n_steps = (t // tm) * nf
    cast_in, cast_out, cast_shapes = [], [], []
    for w in cast_weights:
        br, bc, cc = _chunking(w.shape[1], w.shape[2], n_steps)
        cast_in.append(pl.BlockSpec((None, br, bc), lambda i, f, cc=cc: (cast_layer, (i * nf + f) // cc, (i * nf + f) % cc)))
        cast_out.append(pl.BlockSpec((br, bc), lambda i, f, cc=cc: ((i * nf + f) // cc, (i * nf + f) % cc)))
        cast_shapes.append(jax.ShapeDtypeStruct(w.shape[1:], BF16))
    return pl.pallas_call(
        functools.partial(_mlp_kernel, n_cast=len(cast_weights)),
        grid=(t // tm, nf),
        in_specs=[
            pl.BlockSpec((tm, d), lambda i, f: (i, 0)),
            pl.BlockSpec((1, d), lambda i, f: (0, 0)),
            pl.BlockSpec((d, tf), lambda i, f: (0, f)),
            pl.BlockSpec((tf, d), lambda i, f: (f, 0)),
        ] + cast_in,
        out_specs=[pl.BlockSpec((tm, d), lambda i, f: (i, 0))] + cast_out,
        out_shape=[jax.ShapeDtypeStruct((t, d), F32)] + cast_shapes,
        scratch_shapes=[pltpu.VMEM((tm, d), BF16), pltpu.VMEM((tm, d), F32)],
        compiler_params=pltpu.CompilerParams(
            dimension_semantics=("parallel", "arbitrary"),
            vmem_limit_bytes=VMEM_LIMIT_BYTES),
        name="mlp_residual",
    )(x, g, w_up, w_down, *cast_weights)


def _tile(total, preferred):
    if total <= preferred:
        return total
    for cand in range(preferred, 0, -LANES):
        if total % cand == 0:
            return cand
    return total


def kernel(x, norm_attn_g, w_in, q_norm_g, k_norm_g, sinks, rel_bias, swa_out_g, sb_out_g,
           w_out, norm_mlp_g, w_up, w_down):
    batch, seq, d_model = x.shape
    depth = w_in.shape[0]
    swa_heads = sinks.shape[1]
    swa_q_w = swa_heads * HEAD_DIM
    sb_w = sb_out_g.shape[1]
    sb_heads = sb_w // HEAD_DIM
    d_in = w_in.shape[2]
    swa_kv_w = (d_in - swa_q_w - 3 * sb_w) // 2
    swa_kv_heads = swa_kv_w // HEAD_DIM
    o1 = swa_q_w
    o2 = o1 + swa_kv_w
    o3 = o2 + swa_kv_w
    o4 = o3 + sb_w
    o5 = o4 + sb_w
    assert seq % BLOCK == 0 and swa_kv_heads % 2 == 0 and sb_heads % 2 == 0
    assert swa_heads % swa_kv_heads == 0 and (swa_heads // swa_kv_heads) % 2 == 0

    t = batch * seq
    tm = _tile(t, 512)
    tn_in = _tile(d_in, 1536)
    tm_mlp = tm
    tf = _tile(w_up.shape[2], 1024)
    sb_group = next(g for g in (SB_GROUP, 2, 1) if (seq // BLOCK) % g == 0)

    xt = x.reshape(t, d_model).astype(F32)
    swa_group = next(g for g in (SWA_GROUP, 1) if (seq // BLOCK) % g == 0)
    swa_chunk = next(c for c in (SWA_CHUNK, swa_group) if (seq // BLOCK) % c == 0 and c % swa_group == 0)
    bias = _bias_table(rel_bias, swa_heads // swa_kv_heads)
    row = lambda v: v.reshape(1, -1).astype(F32)

    stacked = (w_in, w_out, w_up, w_down)
    mlp_steps = (t // tm_mlp) * (w_up.shape[2] // tf)
    ride_along = all(_chunking(w.shape[1], w.shape[2], mlp_steps) is not None for w in stacked)
    layer_w = tuple(w[0].astype(BF16) for w in stacked)
    for l in range(depth):
        wl_in, wl_out, wl_up, wl_down = layer_w
        proj = _norm_matmul(xt, row(norm_attn_g[l]), wl_in, tm=tm, tn=tn_in)
        gq2 = row(jnp.tile(q_norm_g[l], LANES // HEAD_DIM))
        gk2 = row(jnp.tile(k_norm_g[l], LANES // HEAD_DIM))
        o_a = _swa_attention(proj, sinks[l].astype(F32), bias, gq2, gk2, batch=batch, seq=seq,
                             n_heads=swa_heads, n_kv_heads=swa_kv_heads, k_col=o1, v_col=o2,
                             group=swa_group, chunk=swa_chunk)
        o_b = _sb_attention(proj, batch=batch, seq=seq, n_heads=sb_heads,
                            q_col=o3, k_col=o4, v_col=o5, tb=BLOCK, group=sb_group)
        xt = _out_proj(o_a, o_b, row(swa_out_g[l]), row(sb_out_g[l]), wl_out, xt, tm=tm)
        if l + 1 < depth and ride_along:
            xt, *layer_w = _mlp(xt, row(norm_mlp_g[l]), wl_up, wl_down, stacked, l + 1, tm=tm_mlp, tf=tf)
        else:
            xt, = _mlp(xt, row(norm_mlp_g[l]), wl_up, wl_down, tm=tm_mlp, tf=tf)
            if l + 1 < depth:
                layer_w = tuple(w[l + 1].astype(BF16) for w in stacked)
    return xt.reshape(batch, seq, d_model).astype(x.dtype)
```

```python
import functools
import math

import numpy as np
import jax
import jax.numpy as jnp
from jax import lax
from jax.experimental import pallas as pl
from jax.experimental.pallas import tpu as pltpu

HEAD_DIM = 64
LANES = 128
BF16_SUBLANES = 16
WINDOW = 128
BLOCK = 128
N_BUCKETS = 32
MAX_DISTANCE = 128
EPS = 1e-6
MASK_VALUE = -1e30
EXP_ZERO_BELOW = -104.0
VMEM_LIMIT_BYTES = 56 * 1024 * 1024
SB_GROUP = 8
SWA_GROUP = 4
SWA_CHUNK = 16

F32 = jnp.float32
BF16 = jnp.bfloat16


def _rms_scale(x):
    return lax.rsqrt(jnp.mean(x * x, axis=-1, keepdims=True) + EPS)


def _norm_matmul_kernel(x_ref, g_ref, w_ref, o_ref, *, tn):
    x = x_ref[...]
    h = (x * _rms_scale(x) * g_ref[...]).astype(BF16)
    for c0 in range(0, o_ref.shape[1], tn):
        o_ref[:, c0:c0 + tn] = jnp.dot(h, w_ref[:, c0:c0 + tn],
                                       preferred_element_type=F32).astype(o_ref.dtype)


def _norm_matmul(x, g, w, *, tm, tn):
    t, d = x.shape
    n = w.shape[1]
    return pl.pallas_call(
        functools.partial(_norm_matmul_kernel, tn=tn),
        grid=(t // tm,),
        in_specs=[
            pl.BlockSpec((tm, d), lambda i: (i, 0)),
            pl.BlockSpec((1, d), lambda i: (0, 0)),
            pl.BlockSpec((d, n), lambda i: (0, 0), pipeline_mode=pl.Buffered(1)),
        ],
        out_specs=pl.BlockSpec((tm, n), lambda i: (i, 0)),
        out_shape=jax.ShapeDtypeStruct((t, n), BF16),
        compiler_params=pltpu.CompilerParams(
            dimension_semantics=("parallel",),
            vmem_limit_bytes=VMEM_LIMIT_BYTES),
        name="norm_in_proj",
    )(x, g, w)


def _t5_bucket_np(dist):
    max_exact = N_BUCKETS // 2
    d = np.maximum(dist, 0)
    ratio = np.maximum(d, 1).astype(np.float32) / max_exact
    large = max_exact + (np.log(ratio) / math.log(MAX_DISTANCE / max_exact)
                         * (N_BUCKETS - max_exact)).astype(np.int32)
    large = np.minimum(large, N_BUCKETS - 1)
    return np.where(d < max_exact, d, large).astype(np.int32)


def _bias_table_kernel(rb_ref, bucket_ref, o_ref):
    h = pl.program_id(0)
    bucket = bucket_ref[...]
    acc = jnp.full(bucket.shape, MASK_VALUE, F32)
    for b in range(N_BUCKETS):
        acc = jnp.where(bucket == b, rb_ref[b, h], acc)
    o_ref[0] = acc


def _bias_table(rel_bias, gqa):
    n_heads = rel_bias.shape[1]
    qi = np.arange(BLOCK)[:, None]
    kj = np.arange(2 * BLOCK)[None, :]
    dist = qi + BLOCK - kj
    in_window = (dist >= 0) & (dist < WINDOW)
    bucket = np.where(in_window, _t5_bucket_np(dist), -1).astype(np.int32)
    return pl.pallas_call(
        _bias_table_kernel,
        grid=(n_heads,),
        in_specs=[
            pl.BlockSpec(memory_space=pltpu.SMEM),
            pl.BlockSpec((BLOCK, 2 * BLOCK), lambda h: (0, 0)),
        ],
        out_specs=pl.BlockSpec((1, BLOCK, 2 * BLOCK), lambda h: (h // gqa, (h % gqa) // 2, h % 2)),
        out_shape=jax.ShapeDtypeStruct((n_heads // gqa, (gqa // 2) * BLOCK, 4 * BLOCK), F32),
        name="t5_bias_table",
    )(rel_bias.astype(F32), jnp.asarray(bucket))


def _head_sumsq(xf, ones_blockdiag):
    return jnp.dot((xf * xf).astype(BF16), ones_blockdiag, preferred_element_type=F32)


def _lane_half_copies(xf, lo_half, fill):
    xr = pltpu.roll(xf, HEAD_DIM, axis=1)
    other = jnp.full_like(xf, fill)
    lo = lambda x: jnp.where(lo_half, x, other).astype(BF16)
    hi = lambda x: jnp.where(lo_half, other, x).astype(BF16)
    return [(lo(xf), hi(xr)), (lo(xr), hi(xf))]


def _swa_kernel(sink_ref, q_ref, kg_ref, kp_ref, vg_ref, vp_ref, bias_ref,
                gq_ref, gk_ref, ones_bd_ref, o_ref, *, gqa, group):
    n_inner = q_ref.shape[0] // (group * BLOCK)
    lax.fori_loop(0, n_inner, lambda s, _: _swa_group(
        s, pl.program_id(2) * n_inner + s, sink_ref, q_ref, kg_ref, kp_ref, vg_ref, vp_ref, bias_ref,
        gq_ref, gk_ref, ones_bd_ref, o_ref, gqa=gqa, group=group), 0)


def _swa_group(s, step, sink_ref, q_ref, kg_ref, kp_ref, vg_ref, vp_ref, bias_ref,
               gq_ref, gk_ref, ones_bd_ref, o_ref, *, gqa, group):
    kvp = pl.program_id(1)
    ppk = gqa // 2
    lo_half = lax.broadcasted_iota(jnp.int32, (1, LANES), 1) < HEAD_DIM
    ones_bd = ones_bd_ref[...]
    nt = (((1,), (1,)), ((), ()))
    scale = 1.0 / math.sqrt(HEAD_DIM)
    base = pl.multiple_of(s * (group * BLOCK), BLOCK)
    before = pl.multiple_of(jnp.maximum(base - BLOCK, 0), BLOCK)

    def with_previous_block(group_ref, prev_ref):
        prev = jnp.where(s == 0, prev_ref[...], group_ref[pl.ds(before, BLOCK), :])
        return jnp.concatenate([prev, group_ref[pl.ds(base, group * BLOCK), :]], axis=0).astype(F32)

    kf = with_previous_block(kg_ref, kp_ref)
    kn = kf * lax.rsqrt(_head_sumsq(kf, ones_bd) * (1.0 / HEAD_DIM) + EPS) * gk_ref[...]
    k_copies = _lane_half_copies(kn, lo_half, 0.0)
    v_copies = _lane_half_copies(with_previous_block(vg_ref, vp_ref), lo_half, 1.0)

    col = lax.broadcasted_iota(jnp.int32, (1, 2 * BLOCK), 1)
    row_top = lax.broadcasted_iota(jnp.int32, (ppk * BLOCK, 1), 0)

    for j in range(group):
        r0 = j * BLOCK
        rows = pl.ds(pl.multiple_of(base + r0, BLOCK), BLOCK)
        key_ok = jnp.logical_or(col >= BLOCK, step * group + j > 0)
        qn = []
        for r in range(2 * ppk):
            qf = q_ref[rows, r * LANES:(r + 1) * LANES].astype(F32)
            qn.append((qf * lax.rsqrt(_head_sumsq(qf, ones_bd) * (1.0 / HEAD_DIM) + EPS)
                       * (gq_ref[...] * scale)).astype(BF16))
        for c in range(2):
            k_lo, k_hi = k_copies[c]
            v_lo, v_hi = v_copies[c]
            kcat = jnp.concatenate([k_lo[r0:r0 + 2 * BLOCK], k_hi[r0:r0 + 2 * BLOCK]], axis=0)
            qs = jnp.concatenate(qn[c * ppk:(c + 1) * ppk], axis=0)
            z = lax.dot_general(qs, kcat, nt, preferred_element_type=F32)
            es, sink_terms = [], []
            for h in range(2):
                sl = slice(h * 2 * BLOCK, (h + 1) * 2 * BLOCK)
                s = jnp.where(key_ok, z[:, sl] + bias_ref[c, :, sl], MASK_VALUE)
                head0 = kvp * 2 * gqa + c * gqa + h
                sink = jnp.full((ppk * BLOCK, 1), sink_ref[head0], F32)
                for r in range(1, ppk):
                    sink = jnp.where(row_top >= r * BLOCK, sink_ref[head0 + 2 * r], sink)
                m = jnp.maximum(jnp.max(s, axis=-1, keepdims=True), sink)
                es.append(jnp.exp(s - m).astype(BF16))
                sink_terms.append(jnp.exp(sink - m))
            res = [jnp.dot(es[0], v_lo[r0:r0 + 2 * BLOCK], preferred_element_type=F32),
                   jnp.dot(es[1], v_hi[r0:r0 + 2 * BLOCK], preferred_element_type=F32)]
            num = jnp.where(lo_half, res[0], res[1])
            den = jnp.where(lo_half, pltpu.roll(res[0], HEAD_DIM, axis=1) + sink_terms[0],
                            pltpu.roll(res[1], HEAD_DIM, axis=1) + sink_terms[1])
            out = (num * (1.0 / den)).astype(o_ref.dtype)
            for r in range(ppk):
                lane0 = (c * ppk + r) * LANES
                o_ref[rows, lane0:lane0 + LANES] = out[r * BLOCK:(r + 1) * BLOCK]
    return 0


def _swa_attention(proj, sinks, bias, gq2, gk2, *, batch, seq, n_heads, n_kv_heads, k_col, v_col, group, chunk):
    t = proj.shape[0]
    nb = seq // BLOCK
    ns = nb // chunk
    kv_pairs = n_kv_heads // 2
    gqa = n_heads // n_kv_heads
    qw = 2 * gqa * HEAD_DIM
    kcb = k_col // LANES
    vcb = v_col // LANES
    ones_bd = jnp.asarray(np.kron(np.eye(2), np.ones((HEAD_DIM, HEAD_DIM))), BF16)

    cur = lambda cb: (lambda b, g, i: (b * ns + i, cb + g))
    prev = lambda cb: (lambda b, g, i: (b * nb + jnp.maximum(i * chunk - 1, 0), cb + g))
    return pl.pallas_call(
        functools.partial(_swa_kernel, gqa=gqa, group=group),
        grid=(batch, kv_pairs, ns),
        in_specs=[
            pl.BlockSpec(memory_space=pltpu.SMEM),
            pl.BlockSpec((chunk * BLOCK, qw), lambda b, g, i: (b * ns + i, g)),
            pl.BlockSpec((chunk * BLOCK, LANES), cur(kcb)),
            pl.BlockSpec((BLOCK, LANES), prev(kcb)),
            pl.BlockSpec((chunk * BLOCK, LANES), cur(vcb)),
            pl.BlockSpec((BLOCK, LANES), prev(vcb)),
            pl.BlockSpec((2, (gqa // 2) * BLOCK, 4 * BLOCK), lambda b, g, i: (g, 0, 0)),
            pl.BlockSpec((1, LANES), lambda b, g, i: (0, 0)),
            pl.BlockSpec((1, LANES), lambda b, g, i: (0, 0)),
            pl.BlockSpec((LANES, LANES), lambda b, g, i: (0, 0)),
        ],
        out_specs=pl.BlockSpec((chunk * BLOCK, qw), lambda b, g, i: (b * ns + i, g)),
        out_shape=jax.ShapeDtypeStruct((t, n_heads * HEAD_DIM), BF16),
        compiler_params=pltpu.CompilerParams(
            dimension_semantics=("parallel", "parallel", "parallel"),
            vmem_limit_bytes=VMEM_LIMIT_BYTES),
        name="swa_attention",
    )(sinks, proj, proj, proj, proj, proj, bias, gq2, gk2, ones_bd)


def _sb_kernel(q_ref, k_ref, v_ref, u_ref, o_ref, acc_ref, carry_ref, z_ref, w_ref, *, tb, group):
    n_groups = q_ref.shape[0] // (group * tb)
    acc_ref[...] = jnp.zeros_like(acc_ref)
    w_ref[...] = jnp.zeros_like(w_ref)

    def body(step, d_prev):
        _sb_finish_group(jnp.maximum(step - 1, 0), d_prev, v_ref, o_ref, acc_ref, w_ref, tb=tb, group=group)
        return _sb_walk_group(step, q_ref, k_ref, v_ref, u_ref, acc_ref, carry_ref, z_ref, w_ref,
                              tb=tb, group=group)

    d_last = lax.fori_loop(0, n_groups, body, jnp.int32(1))
    _sb_finish_group(n_groups - 1, d_last, v_ref, o_ref, acc_ref, w_ref, tb=tb, group=group)


def _split_heads(x2):
    lo_half = lax.broadcasted_iota(jnp.int32, (1, LANES), 1) < HEAD_DIM
    zero = jnp.zeros_like(x2)
    return jnp.concatenate([jnp.where(lo_half, x2, zero), jnp.where(lo_half, zero, x2)], axis=0)


def _sb_finish_group(step, d_end, v_ref, o_ref, acc_ref, w_ref, *, tb, group):
    d = d_end - 1
    for i in range(group):
        start = pl.multiple_of(jnp.maximum(step * group + i - d, 0) * tb, tb)
        acc = acc_ref[i] + jnp.dot(w_ref[d & 1, i], _split_heads(v_ref[pl.ds(start, tb), :]),
                                   preferred_element_type=F32)
        o_ref[pl.ds(pl.multiple_of((step * group + i) * tb, tb), tb), :] = acc.astype(o_ref.dtype)


def _sb_walk_group(step, q_ref, k_ref, v_ref, u_ref, acc_ref, carry_ref, z_ref, w_ref, *, tb, group):
    base = step * (group * tb)
    scale = 1.0 / math.sqrt(HEAD_DIM)
    uu = u_ref[...]
    row = lax.broadcasted_iota(jnp.int32, (tb, tb), 0)
    colk = lax.broadcasted_iota(jnp.int32, (tb, tb), 1)
    below_diag = colk < row
    nt = (((1,), (1,)), ((), ()))
    split_heads = _split_heads

    acc_ref[...] = jnp.zeros_like(acc_ref)
    carry_ref[...] = jnp.zeros_like(carry_ref)

    q_heads = [split_heads((q_ref[pl.ds(pl.multiple_of(base + i * tb, tb), tb), :].astype(F32)
                            * scale).astype(BF16)) for i in range(group)]

    def tile_start(i, d):
        kb = step * group + i - d
        return pl.multiple_of(jnp.maximum(kb, 0) * tb, tb), kb >= 0

    def issue_scores(d):
        for i in range(group):
            start, _ = tile_start(i, d)
            z_ref[d & 1, i] = lax.dot_general(q_heads[i], k_ref[pl.ds(start, tb), :], nt,
                                              preferred_element_type=F32)

    def weights(d, masked):
        log_betas, xs = [], []
        for i in range(group):
            for h in range(2):
                z = z_ref[d & 1, i, h * tb:(h + 1) * tb, :]
                neg_abs = lax.bitcast_convert_type(
                    lax.bitcast_convert_type(z, jnp.int32) | jnp.int32(-2 ** 31), F32)
                soft = jnp.log(1.0 + jnp.exp(neg_abs))
                log_beta = jnp.minimum(z, 0.0) - soft
                log_1m = log_beta - z
                if masked:
                    log_1m = jnp.where(below_diag, log_1m, 0.0)
                log_betas.append(log_beta)
                xs.append(log_1m.astype(BF16))
        cs_all = jnp.dot(jnp.concatenate(xs, axis=0), uu, preferred_element_type=F32)
        worst = None
        for i in range(group):
            _, active = tile_start(i, d)
            for h in range(2):
                n = 2 * i + h
                cs = cs_all[n * tb:(n + 1) * tb]
                carry = jnp.where(active, carry_ref[i, h], MASK_VALUE)
                w = jnp.exp(log_betas[n] + cs[:, :tb] + carry)
                if masked:
                    w = jnp.where(below_diag, w, 0.0)
                w_ref[d & 1, i, :, h * tb:(h + 1) * tb] = w.astype(BF16)
                carry = carry + cs[:, tb:]
                carry_ref[i, h] = carry
                top = jnp.max(carry, axis=0, keepdims=True)[0, 0]
                worst = top if worst is None else jnp.maximum(worst, top)
        return worst

    def accumulate(d):
        for i in range(group):
            start, _ = tile_start(i, d)
            acc_ref[i] += jnp.dot(w_ref[d & 1, i], split_heads(v_ref[pl.ds(start, tb), :]),
                                  preferred_element_type=F32)

    zero = jnp.int32(0)
    issue_scores(zero)
    worst0 = weights(zero, True)
    issue_scores(zero + 1)
    last_block = step * group + group - 1

    def cond(state):
        d, worst = state
        return jnp.logical_and(d <= last_block, worst >= EXP_ZERO_BELOW)

    def body(state):
        d, _ = state
        accumulate(d - 1)
        worst = weights(d, False)
        issue_scores(d + 1)
        return d + 1, worst

    d_end, _ = lax.while_loop(cond, body, (zero + 1, worst0))
    return d_end


def _sb_attention(proj, *, batch, seq, n_heads, q_col, k_col, v_col, tb, group):
    t = proj.shape[0]
    pairs = n_heads // 2
    qcb, kcb, vcb = q_col // LANES, k_col // LANES, v_col // LANES
    tri = np.arange(tb)[:, None] > np.arange(tb)[None, :]
    uu = jnp.asarray(np.concatenate([tri, np.ones((tb, tb), bool)], axis=1), BF16)
    return pl.pallas_call(
        functools.partial(_sb_kernel, tb=tb, group=group),
        grid=(batch, pairs),
        in_specs=[
            pl.BlockSpec((seq, LANES), lambda b, p: (b, qcb + p)),
            pl.BlockSpec((seq, LANES), lambda b, p: (b, kcb + p)),
            pl.BlockSpec((seq, LANES), lambda b, p: (b, vcb + p)),
            pl.BlockSpec((tb, 2 * tb), lambda b, p: (0, 0)),
        ],
        out_specs=pl.BlockSpec((seq, LANES), lambda b, p: (b, p)),
        out_shape=jax.ShapeDtypeStruct((t, n_heads * HEAD_DIM), BF16),
        scratch_shapes=[pltpu.VMEM((group, tb, LANES), F32),
                        pltpu.VMEM((group, 2, tb, tb), F32),
                        pltpu.VMEM((2, group, 2 * tb, tb), F32),
                        pltpu.VMEM((2, group, tb, 2 * tb), BF16)],
        compiler_params=pltpu.CompilerParams(
            dimension_semantics=("parallel", "parallel"),
            vmem_limit_bytes=VMEM_LIMIT_BYTES),
        name="sb_attention",
    )(proj, proj, proj, uu)


def _out_proj_kernel(oa_ref, ob_ref, ga_ref, gb_ref, w_ref, x_ref, o_ref):
    def normed(o_r, g_r):
        o = o_r[...].astype(F32)
        return (o * _rms_scale(o) * g_r[...]).astype(BF16)

    wa = oa_ref.shape[1]
    acc = jnp.dot(normed(oa_ref, ga_ref), w_ref[:wa, :], preferred_element_type=F32)
    acc += jnp.dot(normed(ob_ref, gb_ref), w_ref[wa:, :], preferred_element_type=F32)
    o_ref[...] = x_ref[...] + acc


def _out_proj(o_a, o_b, ga, gb, w, x, *, tm):
    t, d = x.shape
    wa, wb = o_a.shape[1], o_b.shape[1]
    return pl.pallas_call(
        _out_proj_kernel,
        grid=(t // tm,),
        in_specs=[
            pl.BlockSpec((tm, wa), lambda i: (i, 0)),
            pl.BlockSpec((tm, wb), lambda i: (i, 0)),
            pl.BlockSpec((1, wa), lambda i: (0, 0)),
            pl.BlockSpec((1, wb), lambda i: (0, 0)),
            pl.BlockSpec((wa + wb, d), lambda i: (0, 0), pipeline_mode=pl.Buffered(1)),
            pl.BlockSpec((tm, d), lambda i: (i, 0)),
        ],
        out_specs=pl.BlockSpec((tm, d), lambda i: (i, 0)),
        out_shape=jax.ShapeDtypeStruct((t, d), F32),
        compiler_params=pltpu.CompilerParams(
            dimension_semantics=("parallel",),
            vmem_limit_bytes=VMEM_LIMIT_BYTES),
        name="out_proj_residual",
    )(o_a, o_b, ga, gb, w, x)


def _mlp_kernel(x_ref, g_ref, wu_ref, wd_ref, *refs, n_cast):
    cast_src, o_ref, cast_dst = refs[:n_cast], refs[n_cast], refs[n_cast + 1:2 * n_cast + 1]
    h_ref, acc_ref = refs[2 * n_cast + 1:]
    f = pl.program_id(1)
    last = pl.num_programs(1) - 1

    def ffn_slice(h):
        for src, dst in zip(cast_src, cast_dst):
            dst[...] = src[...].astype(dst.dtype)
        u = jnp.maximum(jnp.dot(h, wu_ref[...], preferred_element_type=F32), 0.0)
        return jnp.dot((u * u).astype(BF16), wd_ref[...], preferred_element_type=F32)

    @pl.when(f == 0)
    def _():
        x = x_ref[...]
        h = (x * _rms_scale(x) * g_ref[...]).astype(h_ref.dtype)
        h_ref[...] = h
        acc_ref[...] = ffn_slice(h)

    @pl.when(jnp.logical_and(f > 0, f < last))
    def _():
        acc_ref[...] += ffn_slice(h_ref[...])

    @pl.when(f == last)
    def _():
        o_ref[...] = x_ref[...] + acc_ref[...] + ffn_slice(h_ref[...])


def _chunking(rows, cols, n_chunks):
    for col_chunks in (1, 2, 4, 8, 16):
        row_chunks, rem = divmod(n_chunks, col_chunks)
        if rem or rows % row_chunks or cols % col_chunks:
            continue
        br, bc = rows // row_chunks, cols // col_chunks
        if br % BF16_SUBLANES == 0 and bc % LANES == 0:
            return br, bc, col_chunks
    return None


def _mlp(x, g, w_up, w_down, cast_weights=(), cast_layer=0, *, tm, tf):
    t, d = x.shape
    ff = w_up.shape[1]
    nf = ff // tf
    assert nf >= 2, "first and last d_ff slices are distinct code paths"
    n_steps = (t // tm) * nf
    cast_in, cast_out, cast_shapes = [], [], []
    for w in cast_weights:
        br, bc, cc = _chunking(w.shape[1], w.shape[2], n_steps)
        cast_in.append(pl.BlockSpec((None, br, bc), lambda i, f, cc=cc: (cast_layer, (i * nf + f) // cc, (i * nf + f) % cc)))
        cast_out.append(pl.BlockSpec((br, bc), lambda i, f, cc=cc: ((i * nf + f) // cc, (i * nf + f) % cc)))
        cast_shapes.append(jax.ShapeDtypeStruct(w.shape[1:], BF16))
    return pl.pallas_call(
        functools.partial(_mlp_kernel, n_cast=len(cast_weights)),
        grid=(t // tm, nf),
        in_specs=[
            pl.BlockSpec((tm, d), lambda i, f: (i, 0)),
            pl.BlockSpec((1, d), lambda i, f: (0, 0)),
            pl.BlockSpec((d, tf), lambda i, f: (0, f)),
            pl.BlockSpec((tf, d), lambda i, f: (f, 0)),
        ] + cast_in,
        out_specs=[pl.BlockSpec((tm, d), lambda i, f: (i, 0))] + cast_out,
        out_shape=[jax.ShapeDtypeStruct((t, d), F32)] + cast_shapes,
        scratch_shapes=[pltpu.VMEM((tm, d), BF16), pltpu.VMEM((tm, d), F32)],
        compiler_params=pltpu.CompilerParams(
            dimension_semantics=("parallel", "arbitrary"),
            vmem_limit_bytes=VMEM_LIMIT_BYTES),
        name="mlp_residual",
    )(x, g, w_up, w_down, *cast_weights)


def _tile(total, preferred):
    if total <= preferred:
        return total
    for cand in range(preferred, 0, -LANES):
        if total % cand == 0:
            return cand
    return total


def kernel(x, norm_attn_g, w_in, q_norm_g, k_norm_g, sinks, rel_bias, swa_out_g, sb_out_g,
           w_out, norm_mlp_g, w_up, w_down):
    batch, seq, d_model = x.shape
    depth = w_in.shape[0]
    swa_heads = sinks.shape[1]
    swa_q_w = swa_heads * HEAD_DIM
    sb_w = sb_out_g.shape[1]
    sb_heads = sb_w // HEAD_DIM
    d_in = w_in.shape[2]
    swa_kv_w = (d_in - swa_q_w - 3 * sb_w) // 2
    swa_kv_heads = swa_kv_w // HEAD_DIM
    o1 = swa_q_w
    o2 = o1 + swa_kv_w
    o3 = o2 + swa_kv_w
    o4 = o3 + sb_w
    o5 = o4 + sb_w
    assert seq % BLOCK == 0 and swa_kv_heads % 2 == 0 and sb_heads % 2 == 0
    assert swa_heads % swa_kv_heads == 0 and (swa_heads // swa_kv_heads) % 2 == 0

    t = batch * seq
    tm = _tile(t, 512)
    tn_in = _tile(d_in, 1536)
    tm_mlp = tm
    tf = _tile(w_up.shape[2], 1024)
    sb_group = next(g for g in (SB_GROUP, 2, 1) if (seq // BLOCK) % g == 0)

    xt = x.reshape(t, d_model).astype(F32)
    swa_group = next(g for g in (SWA_GROUP, 1) if (seq // BLOCK) % g == 0)
    swa_chunk = next(c for c in (SWA_CHUNK, swa_group) if (seq // BLOCK) % c == 0 and c % swa_group == 0)
    bias = _bias_table(rel_bias, swa_heads // swa_kv_heads)
    row = lambda v: v.reshape(1, -1).astype(F32)

    stacked = (w_in, w_out, w_up, w_down)
    mlp_steps = (t // tm_mlp) * (w_up.shape[2] // tf)
    ride_along = all(_chunking(w.shape[1], w.shape[2], mlp_steps) is not None for w in stacked)
    layer_w = tuple(w[0].astype(BF16) for w in stacked)
    for l in range(depth):
        wl_in, wl_out, wl_up, wl_down = layer_w
        proj = _norm_matmul(xt, row(norm_attn_g[l]), wl_in, tm=tm, tn=tn_in)
        gq2 = row(jnp.tile(q_norm_g[l], LANES // HEAD_DIM))
        gk2 = row(jnp.tile(k_norm_g[l], LANES // HEAD_DIM))
        o_a = _swa_attention(proj, sinks[l].astype(F32), bias, gq2, gk2, batch=batch, seq=seq,
                             n_heads=swa_heads, n_kv_heads=swa_kv_heads, k_col=o1, v_col=o2,
                             group=swa_group, chunk=swa_chunk)
        o_b = _sb_attention(proj, batch=batch, seq=seq, n_heads=sb_heads,
                            q_col=o3, k_col=o4, v_col=o5, tb=BLOCK, group=sb_group)
        xt = _out_proj(o_a, o_b, row(swa_out_g[l]), row(sb_out_g[l]), wl_out, xt, tm=tm)
        if l + 1 < depth and ride_along:
            xt, *layer_w = _mlp(xt, row(norm_mlp_g[l]), wl_up, wl_down, stacked, l + 1, tm=tm_mlp, tf=tf)
        else:
            xt, = _mlp(xt, row(norm_mlp_g[l]), wl_up, wl_down, tm=tm_mlp, tf=tf)
            if l + 1 < depth:
                layer_w = tuple(w[l + 1].astype(BF16) for w in stacked)
    return xt.reshape(batch, seq, d_model).astype(x.dtype)
```

```python
import functools
import math

import numpy as np
import jax
import jax.numpy as jnp
from jax import lax
from jax.experimental import pallas as pl
from jax.experimental.pallas import tpu as pltpu

HEAD_DIM = 64
LANES = 128
BF16_SUBLANES = 16
WINDOW = 128
BLOCK = 128
N_BUCKETS = 32
MAX_DISTANCE = 128
EPS = 1e-6
MASK_VALUE = -1e30
EXP_ZERO_BELOW = -104.0
VMEM_LIMIT_BYTES = 56 * 1024 * 1024
SB_GROUP = 8
SWA_GROUP = 4
SWA_CHUNK = 16

F32 = jnp.float32
BF16 = jnp.bfloat16


def _rms_scale(x):
    return lax.rsqrt(jnp.mean(x * x, axis=-1, keepdims=True) + EPS)


def _norm_matmul_kernel(x_ref, g_ref, w_ref, o_ref, *, tn):
    x = x_ref[...]
    h = (x * _rms_scale(x) * g_ref[...]).astype(BF16)
    for c0 in range(0, o_ref.shape[1], tn):
        o_ref[:, c0:c0 + tn] = jnp.dot(h, w_ref[:, c0:c0 + tn],
                                       preferred_element_type=F32).astype(o_ref.dtype)


def _norm_matmul(x, g, w, *, tm, tn):
    t, d = x.shape
    n = w.shape[1]
    return pl.pallas_call(
        functools.partial(_norm_matmul_kernel, tn=tn),
        grid=(t // tm,),
        in_specs=[
            pl.BlockSpec((tm, d), lambda i: (i, 0)),
            pl.BlockSpec((1, d), lambda i: (0, 0)),
            pl.BlockSpec((d, n), lambda i: (0, 0), pipeline_mode=pl.Buffered(1)),
        ],
        out_specs=pl.BlockSpec((tm, n), lambda i: (i, 0)),
        out_shape=jax.ShapeDtypeStruct((t, n), BF16),
        compiler_params=pltpu.CompilerParams(
            dimension_semantics=("parallel",),
            vmem_limit_bytes=VMEM_LIMIT_BYTES),
        name="norm_in_proj",
    )(x, g, w)


def _t5_bucket_np(dist):
    max_exact = N_BUCKETS // 2
    d = np.maximum(dist, 0)
    ratio = np.maximum(d, 1).astype(np.float32) / max_exact
    large = max_exact + (np.log(ratio) / math.log(MAX_DISTANCE / max_exact)
                         * (N_BUCKETS - max_exact)).astype(np.int32)
    large = np.minimum(large, N_BUCKETS - 1)
    return np.where(d < max_exact, d, large).astype(np.int32)


def _bias_table_kernel(rb_ref, bucket_ref, o_ref):
    h = pl.program_id(0)
    bucket = bucket_ref[...]
    acc = jnp.full(bucket.shape, MASK_VALUE, F32)
    for b in range(N_BUCKETS):
        acc = jnp.where(bucket == b, rb_ref[b, h], acc)
    o_ref[0] = acc


def _bias_table(rel_bias, gqa):
    n_heads = rel_bias.shape[1]
    qi = np.arange(BLOCK)[:, None]
    kj = np.arange(2 * BLOCK)[None, :]
    dist = qi + BLOCK - kj
    in_window = (dist >= 0) & (dist < WINDOW)
    bucket = np.where(in_window, _t5_bucket_np(dist), -1).astype(np.int32)
    return pl.pallas_call(
        _bias_table_kernel,
        grid=(n_heads,),
        in_specs=[
            pl.BlockSpec(memory_space=pltpu.SMEM),
            pl.BlockSpec((BLOCK, 2 * BLOCK), lambda h: (0, 0)),
        ],
        out_specs=pl.BlockSpec((1, BLOCK, 2 * BLOCK), lambda h: (h // gqa, (h % gqa) // 2, h % 2)),
        out_shape=jax.ShapeDtypeStruct((n_heads // gqa, (gqa // 2) * BLOCK, 4 * BLOCK), F32),
        name="t5_bias_table",
    )(rel_bias.astype(F32), jnp.asarray(bucket))


def _head_sumsq(xf, ones_blockdiag):
    return jnp.dot((xf * xf).astype(BF16), ones_blockdiag, preferred_element_type=F32)


def _lane_half_copies(xf, lo_half, fill):
    xr = pltpu.roll(xf, HEAD_DIM, axis=1)
    other = jnp.full_like(xf, fill)
    lo = lambda x: jnp.where(lo_half, x, other).astype(BF16)
    hi = lambda x: jnp.where(lo_half, other, x).astype(BF16)
    return [(lo(xf), hi(xr)), (lo(xr), hi(xf))]


def _swa_kernel(sink_ref, q_ref, kg_ref, kp_ref, vg_ref, vp_ref, bias_ref,
                gq_ref, gk_ref, ones_bd_ref, o_ref, *, gqa, group):
    n_inner = q_ref.shape[0] // (group * BLOCK)
    lax.fori_loop(0, n_inner, lambda s, _: _swa_group(
        s, pl.program_id(2) * n_inner + s, sink_ref, q_ref, kg_ref, kp_ref, vg_ref, vp_ref, bias_ref,
        gq_ref, gk_ref, ones_bd_ref, o_ref, gqa=gqa, group=group), 0)


def _swa_group(s, step, sink_ref, q_ref, kg_ref, kp_ref, vg_ref, vp_ref, bias_ref,
               gq_ref, gk_ref, ones_bd_ref, o_ref, *, gqa, group):
    kvp = pl.program_id(1)
    ppk = gqa // 2
    lo_half = lax.broadcasted_iota(jnp.int32, (1, LANES), 1) < HEAD_DIM
    ones_bd = ones_bd_ref[...]
    nt = (((1,), (1,)), ((), ()))
    scale = 1.0 / math.sqrt(HEAD_DIM)
    base = pl.multiple_of(s * (group * BLOCK), BLOCK)
    before = pl.multiple_of(jnp.maximum(base - BLOCK, 0), BLOCK)

    def with_previous_block(group_ref, prev_ref):
        prev = jnp.where(s == 0, prev_ref[...], group_ref[pl.ds(before, BLOCK), :])
        return jnp.concatenate([prev, group_ref[pl.ds(base, group * BLOCK), :]], axis=0).astype(F32)

    kf = with_previous_block(kg_ref, kp_ref)
    kn = kf * lax.rsqrt(_head_sumsq(kf, ones_bd) * (1.0 / HEAD_DIM) + EPS) * gk_ref[...]
    k_copies = _lane_half_copies(kn, lo_half, 0.0)
    v_copies = _lane_half_copies(with_previous_block(vg_ref, vp_ref), lo_half, 1.0)

    col = lax.broadcasted_iota(jnp.int32, (1, 2 * BLOCK), 1)
    row_top = lax.broadcasted_iota(jnp.int32, (ppk * BLOCK, 1), 0)

    for j in range(group):
        r0 = j * BLOCK
        rows = pl.ds(pl.multiple_of(base + r0, BLOCK), BLOCK)
        key_ok = jnp.logical_or(col >= BLOCK, step * group + j > 0)
        qn = []
        for r in range(2 * ppk):
            qf = q_ref[rows, r * LANES:(r + 1) * LANES].astype(F32)
            qn.append((qf * lax.rsqrt(_head_sumsq(qf, ones_bd) * (1.0 / HEAD_DIM) + EPS)
                       * (gq_ref[...] * scale)).astype(BF16))
        for c in range(2):
            k_lo, k_hi = k_copies[c]
            v_lo, v_hi = v_copies[c]
            kcat = jnp.concatenate([k_lo[r0:r0 + 2 * BLOCK], k_hi[r0:r0 + 2 * BLOCK]], axis=0)
            qs = jnp.concatenate(qn[c * ppk:(c + 1) * ppk], axis=0)
            z = lax.dot_general(qs, kcat, nt, preferred_element_type=F32)
            es, sink_terms = [], []
            for h in range(2):
                sl = slice(h * 2 * BLOCK, (h + 1) * 2 * BLOCK)
                s = jnp.where(key_ok, z[:, sl] + bias_ref[c, :, sl], MASK_VALUE)
                head0 = kvp * 2 * gqa + c * gqa + h
                sink = jnp.full((ppk * BLOCK, 1), sink_ref[head0], F32)
                for r in range(1, ppk):
                    sink = jnp.where(row_top >= r * BLOCK, sink_ref[head0 + 2 * r], sink)
                m = jnp.maximum(jnp.max(s, axis=-1, keepdims=True), sink)
                es.append(jnp.exp(s - m).astype(BF16))
                sink_terms.append(jnp.exp(sink - m))
            res = [jnp.dot(es[0], v_lo[r0:r0 + 2 * BLOCK], preferred_element_type=F32),
                   jnp.dot(es[1], v_hi[r0:r0 + 2 * BLOCK], preferred_element_type=F32)]
            num = jnp.where(lo_half, res[0], res[1])
            den = jnp.where(lo_half, pltpu.roll(res[0], HEAD_DIM, axis=1) + sink_terms[0],
                            pltpu.roll(res[1], HEAD_DIM, axis=1) + sink_terms[1])
            out = (num * (1.0 / den)).astype(o_ref.dtype)
            for r in range(ppk):
                lane0 = (c * ppk + r) * LANES
                o_ref[rows, lane0:lane0 + LANES] = out[r * BLOCK:(r + 1) * BLOCK]
    return 0


def _swa_attention(proj, sinks, bias, gq2, gk2, *, batch, seq, n_heads, n_kv_heads, k_col, v_col, group, chunk):
    t = proj.shape[0]
    nb = seq // BLOCK
    ns = nb // chunk
    kv_pairs = n_kv_heads // 2
    gqa = n_heads // n_kv_heads
    qw = 2 * gqa * HEAD_DIM
    kcb = k_col // LANES
    vcb = v_col // LANES
    ones_bd = jnp.asarray(np.kron(np.eye(2), np.ones((HEAD_DIM, HEAD_DIM))), BF16)

    cur = lambda cb: (lambda b, g, i: (b * ns + i, cb + g))
    prev = lambda cb: (lambda b, g, i: (b * nb + jnp.maximum(i * chunk - 1, 0), cb + g))
    return pl.pallas_call(
        functools.partial(_swa_kernel, gqa=gqa, group=group),
        grid=(batch, kv_pairs, ns),
        in_specs=[
            pl.BlockSpec(memory_space=pltpu.SMEM),
            pl.BlockSpec((chunk * BLOCK, qw), lambda b, g, i: (b * ns + i, g)),
            pl.BlockSpec((chunk * BLOCK, LANES), cur(kcb)),
            pl.BlockSpec((BLOCK, LANES), prev(kcb)),
            pl.BlockSpec((chunk * BLOCK, LANES), cur(vcb)),
            pl.BlockSpec((BLOCK, LANES), prev(vcb)),
            pl.BlockSpec((2, (gqa // 2) * BLOCK, 4 * BLOCK), lambda b, g, i: (g, 0, 0)),
            pl.BlockSpec((1, LANES), lambda b, g, i: (0, 0)),
            pl.BlockSpec((1, LANES), lambda b, g, i: (0, 0)),
            pl.BlockSpec((LANES, LANES), lambda b, g, i: (0, 0)),
        ],
        out_specs=pl.BlockSpec((chunk * BLOCK, qw), lambda b, g, i: (b * ns + i, g)),
        out_shape=jax.ShapeDtypeStruct((t, n_heads * HEAD_DIM), BF16),
        compiler_params=pltpu.CompilerParams(
            dimension_semantics=("parallel", "parallel", "parallel"),
            vmem_limit_bytes=VMEM_LIMIT_BYTES),
        name="swa_attention",
    )(sinks, proj, proj, proj, proj, proj, bias, gq2, gk2, ones_bd)


def _sb_kernel(q_ref, k_ref, v_ref, u_ref, o_ref, acc_ref, carry_ref, z_ref, w_ref, *, tb, group):
    n_groups = q_ref.shape[0] // (group * tb)
    acc_ref[...] = jnp.zeros_like(acc_ref)
    w_ref[...] = jnp.zeros_like(w_ref)

    def body(step, d_prev):
        _sb_finish_group(jnp.maximum(step - 1, 0), d_prev, v_ref, o_ref, acc_ref, w_ref, tb=tb, group=group)
        return _sb_walk_group(step, q_ref, k_ref, v_ref, u_ref, acc_ref, carry_ref, z_ref, w_ref,
                              tb=tb, group=group)

    d_last = lax.fori_loop(0, n_groups, body, jnp.int32(1))
    _sb_finish_group(n_groups - 1, d_last, v_ref, o_ref, acc_ref, w_ref, tb=tb, group=group)


def _split_heads(x2):
    lo_half = lax.broadcasted_iota(jnp.int32, (1, LANES), 1) < HEAD_DIM
    zero = jnp.zeros_like(x2)
    return jnp.concatenate([jnp.where(lo_half, x2, zero), jnp.where(lo_half, zero, x2)], axis=0)


def _sb_finish_group(step, d_end, v_ref, o_ref, acc_ref, w_ref, *, tb, group):
    d = d_end - 1
    for i in range(group):
        start = pl.multiple_of(jnp.maximum(step * group + i - d, 0) * tb, tb)
        acc = acc_ref[i] + jnp.dot(w_ref[d & 1, i], _split_heads(v_ref[pl.ds(start, tb), :]),
                                   preferred_element_type=F32)
        o_ref[pl.ds(pl.multiple_of((step * group + i) * tb, tb), tb), :] = acc.astype(o_ref.dtype)


def _sb_walk_group(step, q_ref, k_ref, v_ref, u_ref, acc_ref, carry_ref, z_ref, w_ref, *, tb, group):
    base = step * (group * tb)
    scale = 1.0 / math.sqrt(HEAD_DIM)
    uu = u_ref[...]
    row = lax.broadcasted_iota(jnp.int32, (tb, tb), 0)
    colk = lax.broadcasted_iota(jnp.int32, (tb, tb), 1)
    below_diag = colk < row
    nt = (((1,), (1,)), ((), ()))
    split_heads = _split_heads

    acc_ref[...] = jnp.zeros_like(acc_ref)
    carry_ref[...] = jnp.zeros_like(carry_ref)

    q_heads = [split_heads((q_ref[pl.ds(pl.multiple_of(base + i * tb, tb), tb), :].astype(F32)
                            * scale).astype(BF16)) for i in range(group)]

    def tile_start(i, d):
        kb = step * group + i - d
        return pl.multiple_of(jnp.maximum(kb, 0) * tb, tb), kb >= 0

    def issue_scores(d):
        for i in range(group):
            start, _ = tile_start(i, d)
            z_ref[d & 1, i] = lax.dot_general(q_heads[i], k_ref[pl.ds(start, tb), :], nt,
                                              preferred_element_type=F32)

    def weights(d, masked):
        log_betas, xs = [], []
        for i in range(group):
            for h in range(2):
                z = z_ref[d & 1, i, h * tb:(h + 1) * tb, :]
                neg_abs = lax.bitcast_convert_type(
                    lax.bitcast_convert_type(z, jnp.int32) | jnp.int32(-2 ** 31), F32)
                soft = jnp.log(1.0 + jnp.exp(neg_abs))
                log_beta = jnp.minimum(z, 0.0) - soft
                log_1m = log_beta - z
                if masked:
                    log_1m = jnp.where(below_diag, log_1m, 0.0)
                log_betas.append(log_beta)
                xs.append(log_1m.astype(BF16))
        cs_all = jnp.dot(jnp.concatenate(xs, axis=0), uu, preferred_element_type=F32)
        worst = None
        for i in range(group):
            _, active = tile_start(i, d)
            for h in range(2):
                n = 2 * i + h
                cs = cs_all[n * tb:(n + 1) * tb]
                carry = jnp.where(active, carry_ref[i, h], MASK_VALUE)
                w = jnp.exp(log_betas[n] + cs[:, :tb] + carry)
                if masked:
                    w = jnp.where(below_diag, w, 0.0)
                w_ref[d & 1, i, :, h * tb:(h + 1) * tb] = w.astype(BF16)
                carry = carry + cs[:, tb:]
                carry_ref[i, h] = carry
                worst = carry if worst is None else jnp.maximum(worst, carry)
        return jnp.max(worst, axis=0, keepdims=True)[0, 0]

    def accumulate(d):
        for i in range(group):
            start, _ = tile_start(i, d)
            acc_ref[i] += jnp.dot(w_ref[d & 1, i], split_heads(v_ref[pl.ds(start, tb), :]),
                                  preferred_element_type=F32)

    zero = jnp.int32(0)
    issue_scores(zero)
    worst0 = weights(zero, True)
    issue_scores(zero + 1)
    last_block = step * group + group - 1

    def cond(state):
        d, worst = state
        return jnp.logical_and(d <= last_block, worst >= EXP_ZERO_BELOW)

    def body(state):
        d, _ = state
        accumulate(d - 1)
        worst = weights(d, False)
        issue_scores(d + 1)
        return d + 1, worst

    d_end, _ = lax.while_loop(cond, body, (zero + 1, worst0))
    return d_end


def _sb_attention(proj, *, batch, seq, n_heads, q_col, k_col, v_col, tb, group):
    t = proj.shape[0]
    pairs = n_heads // 2
    qcb, kcb, vcb = q_col // LANES, k_col // LANES, v_col // LANES
    tri = np.arange(tb)[:, None] > np.arange(tb)[None, :]
    uu = jnp.asarray(np.concatenate([tri, np.ones((tb, tb), bool)], axis=1), BF16)
    return pl.pallas_call(
        functools.partial(_sb_kernel, tb=tb, group=group),
        grid=(batch, pairs),
        in_specs=[
            pl.BlockSpec((seq, LANES), lambda b, p: (b, qcb + p)),
            pl.BlockSpec((seq, LANES), lambda b, p: (b, kcb + p)),
            pl.BlockSpec((seq, LANES), lambda b, p: (b, vcb + p)),
            pl.BlockSpec((tb, 2 * tb), lambda b, p: (0, 0)),
        ],
        out_specs=pl.BlockSpec((seq, LANES), lambda b, p: (b, p)),
        out_shape=jax.ShapeDtypeStruct((t, n_heads * HEAD_DIM), BF16),
        scratch_shapes=[pltpu.VMEM((group, tb, LANES), F32),
                        pltpu.VMEM((group, 2, tb, tb), F32),
                        pltpu.VMEM((2, group, 2 * tb, tb), F32),
                        pltpu.VMEM((2, group, tb, 2 * tb), BF16)],
        compiler_params=pltpu.CompilerParams(
            dimension_semantics=("parallel", "parallel"),
            vmem_limit_bytes=VMEM_LIMIT_BYTES),
        name="sb_attention",
    )(proj, proj, proj, uu)


def _out_proj_kernel(oa_ref, ob_ref, ga_ref, gb_ref, w_ref, x_ref, o_ref):
    def normed(o_r, g_r):
        o = o_r[...].astype(F32)
        return (o * _rms_scale(o) * g_r[...]).astype(BF16)

    wa = oa_ref.shape[1]
    acc = jnp.dot(normed(oa_ref, ga_ref), w_ref[:wa, :], preferred_element_type=F32)
    acc += jnp.dot(normed(ob_ref, gb_ref), w_ref[wa:, :], preferred_element_type=F32)
    o_ref[...] = x_ref[...] + acc


def _out_proj(o_a, o_b, ga, gb, w, x, *, tm):
    t, d = x.shape
    wa, wb = o_a.shape[1], o_b.shape[1]
    return pl.pallas_call(
        _out_proj_kernel,
        grid=(t // tm,),
        in_specs=[
            pl.BlockSpec((tm, wa), lambda i: (i, 0)),
            pl.BlockSpec((tm, wb), lambda i: (i, 0)),
            pl.BlockSpec((1, wa), lambda i: (0, 0)),
            pl.BlockSpec((1, wb), lambda i: (0, 0)),
            pl.BlockSpec((wa + wb, d), lambda i: (0, 0), pipeline_mode=pl.Buffered(1)),
            pl.BlockSpec((tm, d), lambda i: (i, 0)),
        ],
        out_specs=pl.BlockSpec((tm, d), lambda i: (i, 0)),
        out_shape=jax.ShapeDtypeStruct((t, d), F32),
        compiler_params=pltpu.CompilerParams(
            dimension_semantics=("parallel",),
            vmem_limit_bytes=VMEM_LIMIT_BYTES),
        name="out_proj_residual",
    )(o_a, o_b, ga, gb, w, x)


def _mlp_kernel(x_ref, g_ref, wu_ref, wd_ref, *refs, n_cast):
    cast_src, o_ref, cast_dst = refs[:n_cast], refs[n_cast], refs[n_cast + 1:2 * n_cast + 1]
    h_ref, acc_ref = refs[2 * n_cast + 1:]
    f = pl.program_id(1)
    last = pl.num_programs(1) - 1

    def ffn_slice(h):
        for src, dst in zip(cast_src, cast_dst):
            dst[...] = src[...].astype(dst.dtype)
        u = jnp.maximum(jnp.dot(h, wu_ref[...], preferred_element_type=F32), 0.0)
        return jnp.dot((u * u).astype(BF16), wd_ref[...], preferred_element_type=F32)

    @pl.when(f == 0)
    def _():
        x = x_ref[...]
        h = (x * _rms_scale(x) * g_ref[...]).astype(h_ref.dtype)
        h_ref[...] = h
        acc_ref[...] = ffn_slice(h)

    @pl.when(jnp.logical_and(f > 0, f < last))
    def _():
        acc_ref[...] += ffn_slice(h_ref[...])

    @pl.when(f == last)
    def _():
        o_ref[...] = x_ref[...] + acc_ref[...] + ffn_slice(h_ref[...])


def _chunking(rows, cols, n_chunks):
    for col_chunks in (1, 2, 4, 8, 16):
        row_chunks, rem = divmod(n_chunks, col_chunks)
        if rem or rows % row_chunks or cols % col_chunks:
            continue
        br, bc = rows // row_chunks, cols // col_chunks
        if br % BF16_SUBLANES == 0 and bc % LANES == 0:
            return br, bc, col_chunks
    return None


def _mlp(x, g, w_up, w_down, cast_weights=(), cast_layer=0, *, tm, tf):
    t, d = x.shape
    ff = w_up.shape[1]
    nf = ff // tf
    assert nf >= 2, "first and last d_ff slices are distinct code paths"
    n_steps = (t // tm) * nf
    cast_in, cast_out, cast_shapes = [], [], []
    for w in cast_weights:
        br, bc, cc = _chunking(w.shape[1], w.shape[2], n_steps)
        cast_in.append(pl.BlockSpec((None, br, bc), lambda i, f, cc=cc: (cast_layer, (i * nf + f) // cc, (i * nf + f) % cc)))
        cast_out.append(pl.BlockSpec((br, bc), lambda i, f, cc=cc: ((i * nf + f) // cc, (i * nf + f) % cc)))
        cast_shapes.append(jax.ShapeDtypeStruct(w.shape[1:], BF16))
    return pl.pallas_call(
        functools.partial(_mlp_kernel, n_cast=len(cast_weights)),
        grid=(t // tm, nf),
        in_specs=[
            pl.BlockSpec((tm, d), lambda i, f: (i, 0)),
            pl.BlockSpec((1, d), lambda i, f: (0, 0)),
            pl.BlockSpec((d, tf), lambda i, f: (0, f)),
            pl.BlockSpec((tf, d), lambda i, f: (f, 0)),
        ] + cast_in,
        out_specs=[pl.BlockSpec((tm, d), lambda i, f: (i, 0))] + cast_out,
        out_shape=[jax.ShapeDtypeStruct((t, d), F32)] + cast_shapes,
        scratch_shapes=[pltpu.VMEM((tm, d), BF16), pltpu.VMEM((tm, d), F32)],
        compiler_params=pltpu.CompilerParams(
            dimension_semantics=("parallel", "arbitrary"),
            vmem_limit_bytes=VMEM_LIMIT_BYTES),
        name="mlp_residual",
    )(x, g, w_up, w_down, *cast_weights)


def _tile(total, preferred):
    if total <= preferred:
        return total
    for cand in range(preferred, 0, -LANES):
        if total % cand == 0:
            return cand
    return total


def kernel(x, norm_attn_g, w_in, q_norm_g, k_norm_g, sinks, rel_bias, swa_out_g, sb_out_g,
           w_out, norm_mlp_g, w_up, w_down):
    batch, seq, d_model = x.shape
    depth = w_in.shape[0]
    swa_heads = sinks.shape[1]
    swa_q_w = swa_heads * HEAD_DIM
    sb_w = sb_out_g.shape[1]
    sb_heads = sb_w // HEAD_DIM
    d_in = w_in.shape[2]
    swa_kv_w = (d_in - swa_q_w - 3 * sb_w) // 2
    swa_kv_heads = swa_kv_w // HEAD_DIM
    o1 = swa_q_w
    o2 = o1 + swa_kv_w
    o3 = o2 + swa_kv_w
    o4 = o3 + sb_w
    o5 = o4 + sb_w
    assert seq % BLOCK == 0 and swa_kv_heads % 2 == 0 and sb_heads % 2 == 0
    assert swa_heads % swa_kv_heads == 0 and (swa_heads // swa_kv_heads) % 2 == 0

    t = batch * seq
    tm = _tile(t, 512)
    tn_in = _tile(d_in, 1536)
    tm_mlp = tm
    tf = _tile(w_up.shape[2], 1024)
    sb_group = next(g for g in (SB_GROUP, 2, 1) if (seq // BLOCK) % g == 0)

    xt = x.reshape(t, d_model).astype(F32)
    swa_group = next(g for g in (SWA_GROUP, 1) if (seq // BLOCK) % g == 0)
    swa_chunk = next(c for c in (SWA_CHUNK, swa_group) if (seq // BLOCK) % c == 0 and c % swa_group == 0)
    bias = _bias_table(rel_bias, swa_heads // swa_kv_heads)
    row = lambda v: v.reshape(1, -1).astype(F32)

    stacked = (w_in, w_out, w_up, w_down)
    mlp_steps = (t // tm_mlp) * (w_up.shape[2] // tf)
    ride_along = all(_chunking(w.shape[1], w.shape[2], mlp_steps) is not None for w in stacked)
    layer_w = tuple(w[0].astype(BF16) for w in stacked)
    for l in range(depth):
        wl_in, wl_out, wl_up, wl_down = layer_w
        proj = _norm_matmul(xt, row(norm_attn_g[l]), wl_in, tm=tm, tn=tn_in)
        gq2 = row(jnp.tile(q_norm_g[l], LANES // HEAD_DIM))
        gk2 = row(jnp.tile(k_norm_g[l], LANES // HEAD_DIM))
        o_a = _swa_attention(proj, sinks[l].astype(F32), bias, gq2, gk2, batch=batch, seq=seq,
                             n_heads=swa_heads, n_kv_heads=swa_kv_heads, k_col=o1, v_col=o2,
                             group=swa_group, chunk=swa_chunk)
        o_b = _sb_attention(proj, batch=batch, seq=seq, n_heads=sb_heads,
                            q_col=o3, k_col=o4, v_col=o5, tb=BLOCK, group=sb_group)
        xt = _out_proj(o_a, o_b, row(swa_out_g[l]), row(sb_out_g[l]), wl_out, xt, tm=tm)
        if l + 1 < depth and ride_along:
            xt, *layer_w = _mlp(xt, row(norm_mlp_g[l]), wl_up, wl_down, stacked, l + 1, tm=tm_mlp, tf=tf)
        else:
            xt, = _mlp(xt, row(norm_mlp_g[l]), wl_up, wl_down, tm=tm_mlp, tf=tf)
            if l + 1 < depth:
                layer_w = tuple(w[l + 1].astype(BF16) for w in stacked)
    return xt.reshape(batch, seq, d_model).astype(x.dtype)
```

```python
import functools
import math

import numpy as np
import jax
import jax.numpy as jnp
from jax import lax
from jax.experimental import pallas as pl
from jax.experimental.pallas import tpu as pltpu

HEAD_DIM = 64
LANES = 128
BF16_SUBLANES = 16
WINDOW = 128
BLOCK = 128
N_BUCKETS = 32
MAX_DISTANCE = 128
EPS = 1e-6
MASK_VALUE = -1e30
EXP_ZERO_BELOW = -104.0
VMEM_LIMIT_BYTES = 56 * 1024 * 1024
SB_GROUP = 8
SWA_GROUP = 4
SWA_CHUNK = 16

F32 = jnp.float32
BF16 = jnp.bfloat16


def _rms_scale(x):
    return lax.rsqrt(jnp.mean(x * x, axis=-1, keepdims=True) + EPS)


def _norm_matmul_kernel(x_ref, g_ref, w_ref, o_ref, *, tn):
    x = x_ref[...]
    h = (x * _rms_scale(x) * g_ref[...]).astype(BF16)
    for c0 in range(0, o_ref.shape[1], tn):
        o_ref[:, c0:c0 + tn] = jnp.dot(h, w_ref[:, c0:c0 + tn],
                                       preferred_element_type=F32).astype(o_ref.dtype)


def _norm_matmul(x, g, w, *, tm, tn):
    t, d = x.shape
    n = w.shape[1]
    return pl.pallas_call(
        functools.partial(_norm_matmul_kernel, tn=tn),
        grid=(t // tm,),
        in_specs=[
            pl.BlockSpec((tm, d), lambda i: (i, 0)),
            pl.BlockSpec((1, d), lambda i: (0, 0)),
            pl.BlockSpec((d, n), lambda i: (0, 0), pipeline_mode=pl.Buffered(1)),
        ],
        out_specs=pl.BlockSpec((tm, n), lambda i: (i, 0)),
        out_shape=jax.ShapeDtypeStruct((t, n), BF16),
        compiler_params=pltpu.CompilerParams(
            dimension_semantics=("parallel",),
            vmem_limit_bytes=VMEM_LIMIT_BYTES),
        name="norm_in_proj",
    )(x, g, w)


def _t5_bucket_np(dist):
    max_exact = N_BUCKETS // 2
    d = np.maximum(dist, 0)
    ratio = np.maximum(d, 1).astype(np.float32) / max_exact
    large = max_exact + (np.log(ratio) / math.log(MAX_DISTANCE / max_exact)
                         * (N_BUCKETS - max_exact)).astype(np.int32)
    large = np.minimum(large, N_BUCKETS - 1)
    return np.where(d < max_exact, d, large).astype(np.int32)


def _bias_table_kernel(rb_ref, bucket_ref, o_ref):
    h = pl.program_id(0)
    bucket = bucket_ref[...]
    acc = jnp.full(bucket.shape, MASK_VALUE, F32)
    for b in range(N_BUCKETS):
        acc = jnp.where(bucket == b, rb_ref[b, h], acc)
    o_ref[0] = acc


def _bias_table(rel_bias, gqa):
    n_heads = rel_bias.shape[1]
    qi = np.arange(BLOCK)[:, None]
    kj = np.arange(2 * BLOCK)[None, :]
    dist = qi + BLOCK - kj
    in_window = (dist >= 0) & (dist < WINDOW)
    bucket = np.where(in_window, _t5_bucket_np(dist), -1).astype(np.int32)
    return pl.pallas_call(
        _bias_table_kernel,
        grid=(n_heads,),
        in_specs=[
            pl.BlockSpec(memory_space=pltpu.SMEM),
            pl.BlockSpec((BLOCK, 2 * BLOCK), lambda h: (0, 0)),
        ],
        out_specs=pl.BlockSpec((1, BLOCK, 2 * BLOCK), lambda h: (h // gqa, (h % gqa) // 2, h % 2)),
        out_shape=jax.ShapeDtypeStruct((n_heads // gqa, (gqa // 2) * BLOCK, 4 * BLOCK), F32),
        name="t5_bias_table",
    )(rel_bias.astype(F32), jnp.asarray(bucket))


def _head_sumsq(xf, ones_blockdiag):
    return jnp.dot((xf * xf).astype(BF16), ones_blockdiag, preferred_element_type=F32)


def _lane_half_copies(xf, lo_half, fill):
    xr = pltpu.roll(xf, HEAD_DIM, axis=1)
    other = jnp.full_like(xf, fill)
    lo = lambda x: jnp.where(lo_half, x, other).astype(BF16)
    hi = lambda x: jnp.where(lo_half, other, x).astype(BF16)
    return [(lo(xf), hi(xr)), (lo(xr), hi(xf))]


def _swa_kernel(sink_ref, q_ref, kg_ref, kp_ref, vg_ref, vp_ref, bias_ref,
                gq_ref, gk_ref, ones_bd_ref, o_ref, *, gqa, group):
    n_inner = q_ref.shape[0] // (group * BLOCK)
    lax.fori_loop(0, n_inner, lambda s, _: _swa_group(
        s, pl.program_id(2) * n_inner + s, sink_ref, q_ref, kg_ref, kp_ref, vg_ref, vp_ref, bias_ref,
        gq_ref, gk_ref, ones_bd_ref, o_ref, gqa=gqa, group=group), 0)


def _swa_group(s, step, sink_ref, q_ref, kg_ref, kp_ref, vg_ref, vp_ref, bias_ref,
               gq_ref, gk_ref, ones_bd_ref, o_ref, *, gqa, group):
    kvp = pl.program_id(1)
    ppk = gqa // 2
    lo_half = lax.broadcasted_iota(jnp.int32, (1, LANES), 1) < HEAD_DIM
    ones_bd = ones_bd_ref[...]
    nt = (((1,), (1,)), ((), ()))
    scale = 1.0 / math.sqrt(HEAD_DIM)
    base = pl.multiple_of(s * (group * BLOCK), BLOCK)
    before = pl.multiple_of(jnp.maximum(base - BLOCK, 0), BLOCK)

    def with_previous_block(group_ref, prev_ref):
        prev = jnp.where(s == 0, prev_ref[...], group_ref[pl.ds(before, BLOCK), :])
        return jnp.concatenate([prev, group_ref[pl.ds(base, group * BLOCK), :]], axis=0).astype(F32)

    kf = with_previous_block(kg_ref, kp_ref)
    kn = kf * lax.rsqrt(_head_sumsq(kf, ones_bd) * (1.0 / HEAD_DIM) + EPS) * gk_ref[...]
    k_copies = _lane_half_copies(kn, lo_half, 0.0)
    v_copies = _lane_half_copies(with_previous_block(vg_ref, vp_ref), lo_half, 1.0)

    col = lax.broadcasted_iota(jnp.int32, (1, 2 * BLOCK), 1)
    row_top = lax.broadcasted_iota(jnp.int32, (ppk * BLOCK, 1), 0)

    for j in range(group):
        r0 = j * BLOCK
        rows = pl.ds(pl.multiple_of(base + r0, BLOCK), BLOCK)
        key_ok = jnp.logical_or(col >= BLOCK, step * group + j > 0)
        qn = []
        for r in range(2 * ppk):
            qf = q_ref[rows, r * LANES:(r + 1) * LANES].astype(F32)
            qn.append((qf * lax.rsqrt(_head_sumsq(qf, ones_bd) * (1.0 / HEAD_DIM) + EPS)
                       * (gq_ref[...] * scale)).astype(BF16))
        for c in range(2):
            k_lo, k_hi = k_copies[c]
            v_lo, v_hi = v_copies[c]
            kcat = jnp.concatenate([k_lo[r0:r0 + 2 * BLOCK], k_hi[r0:r0 + 2 * BLOCK]], axis=0)
            qs = jnp.concatenate(qn[c * ppk:(c + 1) * ppk], axis=0)
            z = lax.dot_general(qs, kcat, nt, preferred_element_type=F32)
            es, sink_terms = [], []
            for h in range(2):
                sl = slice(h * 2 * BLOCK, (h + 1) * 2 * BLOCK)
                s = jnp.where(key_ok, z[:, sl] + bias_ref[c, :, sl], MASK_VALUE)
                head0 = kvp * 2 * gqa + c * gqa + h
                sink = jnp.full((ppk * BLOCK, 1), sink_ref[head0], F32)
                for r in range(1, ppk):
                    sink = jnp.where(row_top >= r * BLOCK, sink_ref[head0 + 2 * r], sink)
                m = jnp.maximum(jnp.max(s, axis=-1, keepdims=True), sink)
                es.append(jnp.exp(s - m).astype(BF16))
                sink_terms.append(jnp.exp(sink - m))
            res = [jnp.dot(es[0], v_lo[r0:r0 + 2 * BLOCK], preferred_element_type=F32),
                   jnp.dot(es[1], v_hi[r0:r0 + 2 * BLOCK], preferred_element_type=F32)]
            num = jnp.where(lo_half, res[0], res[1])
            den = (pltpu.roll(jnp.where(lo_half, res[1], res[0]), HEAD_DIM, axis=1)
                   + jnp.where(lo_half, sink_terms[0], sink_terms[1]))
            out = (num * (1.0 / den)).astype(o_ref.dtype)
            for r in range(ppk):
                lane0 = (c * ppk + r) * LANES
                o_ref[rows, lane0:lane0 + LANES] = out[r * BLOCK:(r + 1) * BLOCK]
    return 0


def _swa_attention(proj, sinks, bias, gq2, gk2, *, batch, seq, n_heads, n_kv_heads, k_col, v_col, group, chunk):
    t = proj.shape[0]
    nb = seq // BLOCK
    ns = nb // chunk
    kv_pairs = n_kv_heads // 2
    gqa = n_heads // n_kv_heads
    qw = 2 * gqa * HEAD_DIM
    kcb = k_col // LANES
    vcb = v_col // LANES
    ones_bd = jnp.asarray(np.kron(np.eye(2), np.ones((HEAD_DIM, HEAD_DIM))), BF16)

    cur = lambda cb: (lambda b, g, i: (b * ns + i, cb + g))
    prev = lambda cb: (lambda b, g, i: (b * nb + jnp.maximum(i * chunk - 1, 0), cb + g))
    return pl.pallas_call(
        functools.partial(_swa_kernel, gqa=gqa, group=group),
        grid=(batch, kv_pairs, ns),
        in_specs=[
            pl.BlockSpec(memory_space=pltpu.SMEM),
            pl.BlockSpec((chunk * BLOCK, qw), lambda b, g, i: (b * ns + i, g)),
            pl.BlockSpec((chunk * BLOCK, LANES), cur(kcb)),
            pl.BlockSpec((BLOCK, LANES), prev(kcb)),
            pl.BlockSpec((chunk * BLOCK, LANES), cur(vcb)),
            pl.BlockSpec((BLOCK, LANES), prev(vcb)),
            pl.BlockSpec((2, (gqa // 2) * BLOCK, 4 * BLOCK), lambda b, g, i: (g, 0, 0)),
            pl.BlockSpec((1, LANES), lambda b, g, i: (0, 0)),
            pl.BlockSpec((1, LANES), lambda b, g, i: (0, 0)),
            pl.BlockSpec((LANES, LANES), lambda b, g, i: (0, 0)),
        ],
        out_specs=pl.BlockSpec((chunk * BLOCK, qw), lambda b, g, i: (b * ns + i, g)),
        out_shape=jax.ShapeDtypeStruct((t, n_heads * HEAD_DIM), BF16),
        compiler_params=pltpu.CompilerParams(
            dimension_semantics=("parallel", "parallel", "parallel"),
            vmem_limit_bytes=VMEM_LIMIT_BYTES),
        name="swa_attention",
    )(sinks, proj, proj, proj, proj, proj, bias, gq2, gk2, ones_bd)


def _sb_kernel(q_ref, k_ref, v_ref, u_ref, o_ref, acc_ref, carry_ref, z_ref, w_ref, *, tb, group):
    n_groups = q_ref.shape[0] // (group * tb)
    acc_ref[...] = jnp.zeros_like(acc_ref)
    w_ref[...] = jnp.zeros_like(w_ref)

    def body(step, d_prev):
        _sb_finish_group(jnp.maximum(step - 1, 0), d_prev, v_ref, o_ref, acc_ref, w_ref, tb=tb, group=group)
        return _sb_walk_group(step, q_ref, k_ref, v_ref, u_ref, acc_ref, carry_ref, z_ref, w_ref,
                              tb=tb, group=group)

    d_last = lax.fori_loop(0, n_groups, body, jnp.int32(1))
    _sb_finish_group(n_groups - 1, d_last, v_ref, o_ref, acc_ref, w_ref, tb=tb, group=group)


def _split_heads(x2):
    lo_half = lax.broadcasted_iota(jnp.int32, (1, LANES), 1) < HEAD_DIM
    zero = jnp.zeros_like(x2)
    return jnp.concatenate([jnp.where(lo_half, x2, zero), jnp.where(lo_half, zero, x2)], axis=0)


def _sb_finish_group(step, d_end, v_ref, o_ref, acc_ref, w_ref, *, tb, group):
    d = d_end - 1
    for i in range(group):
        start = pl.multiple_of(jnp.maximum(step * group + i - d, 0) * tb, tb)
        acc = acc_ref[i] + jnp.dot(w_ref[d & 1, i], _split_heads(v_ref[pl.ds(start, tb), :]),
                                   preferred_element_type=F32)
        o_ref[pl.ds(pl.multiple_of((step * group + i) * tb, tb), tb), :] = acc.astype(o_ref.dtype)


def _sb_walk_group(step, q_ref, k_ref, v_ref, u_ref, acc_ref, carry_ref, z_ref, w_ref, *, tb, group):
    base = step * (group * tb)
    scale = 1.0 / math.sqrt(HEAD_DIM)
    uu = u_ref[...]
    row = lax.broadcasted_iota(jnp.int32, (tb, tb), 0)
    colk = lax.broadcasted_iota(jnp.int32, (tb, tb), 1)
    below_diag = colk < row
    nt = (((1,), (1,)), ((), ()))
    split_heads = _split_heads

    acc_ref[...] = jnp.zeros_like(acc_ref)
    carry_ref[...] = jnp.zeros_like(carry_ref)

    q_heads = [split_heads((q_ref[pl.ds(pl.multiple_of(base + i * tb, tb), tb), :].astype(F32)
                            * scale).astype(BF16)) for i in range(group)]

    def tile_start(i, d):
        kb = step * group + i - d
        return pl.multiple_of(jnp.maximum(kb, 0) * tb, tb), kb >= 0

    def issue_scores(d):
        for i in range(group):
            start, _ = tile_start(i, d)
            z_ref[d & 1, i] = lax.dot_general(q_heads[i], k_ref[pl.ds(start, tb), :], nt,
                                              preferred_element_type=F32)

    def weights(d, masked):
        log_betas, xs = [], []
        for i in range(group):
            for h in range(2):
                z = z_ref[d & 1, i, h * tb:(h + 1) * tb, :]
                neg_abs = lax.bitcast_convert_type(
                    lax.bitcast_convert_type(z, jnp.int32) | jnp.int32(-2 ** 31), F32)
                soft = jnp.log(1.0 + jnp.exp(neg_abs))
                log_beta = jnp.minimum(z, 0.0) - soft
                log_1m = log_beta - z
                if masked:
                    log_1m = jnp.where(below_diag, log_1m, 0.0)
                log_betas.append(log_beta)
                xs.append(log_1m.astype(BF16))
        cs_all = jnp.dot(jnp.concatenate(xs, axis=0), uu, preferred_element_type=F32)
        worst = None
        for i in range(group):
            _, active = tile_start(i, d)
            for h in range(2):
                n = 2 * i + h
                cs = cs_all[n * tb:(n + 1) * tb]
                carry = jnp.where(active, carry_ref[i, h], MASK_VALUE)
                w = jnp.exp(log_betas[n] + cs[:, :tb] + carry)
                if masked:
                    w = jnp.where(below_diag, w, 0.0)
                w_ref[d & 1, i, :, h * tb:(h + 1) * tb] = w.astype(BF16)
                carry = carry + cs[:, tb:]
                carry_ref[i, h] = carry
                worst = carry if worst is None else jnp.maximum(worst, carry)
        return jnp.max(worst, axis=0, keepdims=True)[0, 0]

    def accumulate(d):
        for i in range(group):
            start, _ = tile_start(i, d)
            acc_ref[i] += jnp.dot(w_ref[d & 1, i], split_heads(v_ref[pl.ds(start, tb), :]),
                                  preferred_element_type=F32)

    zero = jnp.int32(0)
    issue_scores(zero)
    worst0 = weights(zero, True)
    issue_scores(zero + 1)
    last_block = step * group + group - 1

    def cond(state):
        d, worst = state
        return jnp.logical_and(d <= last_block, worst >= EXP_ZERO_BELOW)

    def body(state):
        d, _ = state
        accumulate(d - 1)
        worst = weights(d, False)
        issue_scores(d + 1)
        return d + 1, worst

    d_end, _ = lax.while_loop(cond, body, (zero + 1, worst0))
    return d_end


def _sb_attention(proj, *, batch, seq, n_heads, q_col, k_col, v_col, tb, group):
    t = proj.shape[0]
    pairs = n_heads // 2
    qcb, kcb, vcb = q_col // LANES, k_col // LANES, v_col // LANES
    tri = np.arange(tb)[:, None] > np.arange(tb)[None, :]
    uu = jnp.asarray(np.concatenate([tri, np.ones((tb, tb), bool)], axis=1), BF16)
    return pl.pallas_call(
        functools.partial(_sb_kernel, tb=tb, group=group),
        grid=(batch, pairs),
        in_specs=[
            pl.BlockSpec((seq, LANES), lambda b, p: (b, qcb + p)),
            pl.BlockSpec((seq, LANES), lambda b, p: (b, kcb + p)),
            pl.BlockSpec((seq, LANES), lambda b, p: (b, vcb + p)),
            pl.BlockSpec((tb, 2 * tb), lambda b, p: (0, 0)),
        ],
        out_specs=pl.BlockSpec((seq, LANES), lambda b, p: (b, p)),
        out_shape=jax.ShapeDtypeStruct((t, n_heads * HEAD_DIM), BF16),
        scratch_shapes=[pltpu.VMEM((group, tb, LANES), F32),
                        pltpu.VMEM((group, 2, tb, tb), F32),
                        pltpu.VMEM((2, group, 2 * tb, tb), F32),
                        pltpu.VMEM((2, group, tb, 2 * tb), BF16)],
        compiler_params=pltpu.CompilerParams(
            dimension_semantics=("parallel", "parallel"),
            vmem_limit_bytes=VMEM_LIMIT_BYTES),
        name="sb_attention",
    )(proj, proj, proj, uu)


def _out_proj_kernel(oa_ref, ob_ref, ga_ref, gb_ref, w_ref, x_ref, o_ref):
    def normed(o_r, g_r):
        o = o_r[...].astype(F32)
        return (o * _rms_scale(o) * g_r[...]).astype(BF16)

    wa = oa_ref.shape[1]
    acc = jnp.dot(normed(oa_ref, ga_ref), w_ref[:wa, :], preferred_element_type=F32)
    acc += jnp.dot(normed(ob_ref, gb_ref), w_ref[wa:, :], preferred_element_type=F32)
    o_ref[...] = x_ref[...] + acc


def _out_proj(o_a, o_b, ga, gb, w, x, *, tm):
    t, d = x.shape
    wa, wb = o_a.shape[1], o_b.shape[1]
    return pl.pallas_call(
        _out_proj_kernel,
        grid=(t // tm,),
        in_specs=[
            pl.BlockSpec((tm, wa), lambda i: (i, 0)),
            pl.BlockSpec((tm, wb), lambda i: (i, 0)),
            pl.BlockSpec((1, wa), lambda i: (0, 0)),
            pl.BlockSpec((1, wb), lambda i: (0, 0)),
            pl.BlockSpec((wa + wb, d), lambda i: (0, 0), pipeline_mode=pl.Buffered(1)),
            pl.BlockSpec((tm, d), lambda i: (i, 0)),
        ],
        out_specs=pl.BlockSpec((tm, d), lambda i: (i, 0)),
        out_shape=jax.ShapeDtypeStruct((t, d), F32),
        compiler_params=pltpu.CompilerParams(
            dimension_semantics=("parallel",),
            vmem_limit_bytes=VMEM_LIMIT_BYTES),
        name="out_proj_residual",
    )(o_a, o_b, ga, gb, w, x)


def _mlp_kernel(x_ref, g_ref, wu_ref, wd_ref, *refs, n_cast):
    cast_src, o_ref, cast_dst = refs[:n_cast], refs[n_cast], refs[n_cast + 1:2 * n_cast + 1]
    h_ref, acc_ref = refs[2 * n_cast + 1:]
    f = pl.program_id(1)
    last = pl.num_programs(1) - 1

    def ffn_slice(h):
        for src, dst in zip(cast_src, cast_dst):
            dst[...] = src[...].astype(dst.dtype)
        u = jnp.maximum(jnp.dot(h, wu_ref[...], preferred_element_type=F32), 0.0)
        return jnp.dot((u * u).astype(BF16), wd_ref[...], preferred_element_type=F32)

    @pl.when(f == 0)
    def _():
        x = x_ref[...]
        h = (x * _rms_scale(x) * g_ref[...]).astype(h_ref.dtype)
        h_ref[...] = h
        acc_ref[...] = ffn_slice(h)

    @pl.when(jnp.logical_and(f > 0, f < last))
    def _():
        acc_ref[...] += ffn_slice(h_ref[...])

    @pl.when(f == last)
    def _():
        o_ref[...] = x_ref[...] + acc_ref[...] + ffn_slice(h_ref[...])


def _chunking(rows, cols, n_chunks):
    for col_chunks in (1, 2, 4, 8, 16):
        row_chunks, rem = divmod(n_chunks, col_chunks)
        if rem or rows % row_chunks or cols % col_chunks:
            continue
        br, bc = rows // row_chunks, cols // col_chunks
        if br % BF16_SUBLANES == 0 and bc % LANES == 0:
            return br, bc, col_chunks
    return None


def _mlp(x, g, w_up, w_down, cast_weights=(), cast_layer=0, *, tm, tf):
    t, d = x.shape
    ff = w_up.shape[1]
    nf = ff // tf
    assert nf >= 2, "first and last d_ff slices are distinct code paths"
    n_steps = (t // tm) * nf
    cast_in, cast_out, cast_shapes = [], [], []
    for w in cast_weights:
        br, bc, cc = _chunking(w.shape[1], w.shape[2], n_steps)
        cast_in.append(pl.BlockSpec((None, br, bc), lambda i, f, cc=cc: (cast_layer, (i * nf + f) // cc, (i * nf + f) % cc)))
        cast_out.append(pl.BlockSpec((br, bc), lambda i, f, cc=cc: ((i * nf + f) // cc, (i * nf + f) % cc)))
        cast_shapes.append(jax.ShapeDtypeStruct(w.shape[1:], BF16))
    return pl.pallas_call(
        functools.partial(_mlp_kernel, n_cast=len(cast_weights)),
        grid=(t // tm, nf),
        in_specs=[
            pl.BlockSpec((tm, d), lambda i, f: (i, 0)),
            pl.BlockSpec((1, d), lambda i, f: (0, 0)),
            pl.BlockSpec((d, tf), lambda i, f: (0, f)),
            pl.BlockSpec((tf, d), lambda i, f: (f, 0)),
        ] + cast_in,
        out_specs=[pl.BlockSpec((tm, d), lambda i, f: (i, 0))] + cast_out,
        out_shape=[jax.ShapeDtypeStruct((t, d), F32)] + cast_shapes,
        scratch_shapes=[pltpu.VMEM((tm, d), BF16), pltpu.VMEM((tm, d), F32)],
        compiler_params=pltpu.CompilerParams(
            dimension_semantics=("parallel", "arbitrary"),
            vmem_limit_bytes=VMEM_LIMIT_BYTES),
        name="mlp_residual",
    )(x, g, w_up, w_down, *cast_weights)


def _tile(total, preferred):
    if total <= preferred:
        return total
    for cand in range(preferred, 0, -LANES):
        if total % cand == 0:
            return cand
    return total


def kernel(x, norm_attn_g, w_in, q_norm_g, k_norm_g, sinks, rel_bias, swa_out_g, sb_out_g,
           w_out, norm_mlp_g, w_up, w_down):
    batch, seq, d_model = x.shape
    depth = w_in.shape[0]
    swa_heads = sinks.shape[1]
    swa_q_w = swa_heads * HEAD_DIM
    sb_w = sb_out_g.shape[1]
    sb_heads = sb_w // HEAD_DIM
    d_in = w_in.shape[2]
    swa_kv_w = (d_in - swa_q_w - 3 * sb_w) // 2
    swa_kv_heads = swa_kv_w // HEAD_DIM
    o1 = swa_q_w
    o2 = o1 + swa_kv_w
    o3 = o2 + swa_kv_w
    o4 = o3 + sb_w
    o5 = o4 + sb_w
    assert seq % BLOCK == 0 and swa_kv_heads % 2 == 0 and sb_heads % 2 == 0
    assert swa_heads % swa_kv_heads == 0 and (swa_heads // swa_kv_heads) % 2 == 0

    t = batch * seq
    tm = _tile(t, 512)
    tn_in = _tile(d_in, 1536)
    tm_mlp = tm
    tf = _tile(w_up.shape[2], 1024)
    sb_group = next(g for g in (SB_GROUP, 2, 1) if (seq // BLOCK) % g == 0)

    xt = x.reshape(t, d_model).astype(F32)
    swa_group = next(g for g in (SWA_GROUP, 1) if (seq // BLOCK) % g == 0)
    swa_chunk = next(c for c in (SWA_CHUNK, swa_group) if (seq // BLOCK) % c == 0 and c % swa_group == 0)
    bias = _bias_table(rel_bias, swa_heads // swa_kv_heads)
    row = lambda v: v.reshape(1, -1).astype(F32)

    stacked = (w_in, w_out, w_up, w_down)
    mlp_steps = (t // tm_mlp) * (w_up.shape[2] // tf)
    ride_along = all(_chunking(w.shape[1], w.shape[2], mlp_steps) is not None for w in stacked)
    layer_w = tuple(w[0].astype(BF16) for w in stacked)
    for l in range(depth):
        wl_in, wl_out, wl_up, wl_down = layer_w
        proj = _norm_matmul(xt, row(norm_attn_g[l]), wl_in, tm=tm, tn=tn_in)
        gq2 = row(jnp.tile(q_norm_g[l], LANES // HEAD_DIM))
        gk2 = row(jnp.tile(k_norm_g[l], LANES // HEAD_DIM))
        o_a = _swa_attention(proj, sinks[l].astype(F32), bias, gq2, gk2, batch=batch, seq=seq,
                             n_heads=swa_heads, n_kv_heads=swa_kv_heads, k_col=o1, v_col=o2,
                             group=swa_group, chunk=swa_chunk)
        o_b = _sb_attention(proj, batch=batch, seq=seq, n_heads=sb_heads,
                            q_col=o3, k_col=o4, v_col=o5, tb=BLOCK, group=sb_group)
        xt = _out_proj(o_a, o_b, row(swa_out_g[l]), row(sb_out_g[l]), wl_out, xt, tm=tm)
        if l + 1 < depth and ride_along:
            xt, *layer_w = _mlp(xt, row(norm_mlp_g[l]), wl_up, wl_down, stacked, l + 1, tm=tm_mlp, tf=tf)
        else:
            xt, = _mlp(xt, row(norm_mlp_g[l]), wl_up, wl_down, tm=tm_mlp, tf=tf)
            if l + 1 < depth:
                layer_w = tuple(w[l + 1].astype(BF16) for w in stacked)
    return xt.reshape(batch, seq, d_model).astype(x.dtype)
```

```python
import functools
import math

import numpy as np
import jax
import jax.numpy as jnp
from jax import lax
from jax.experimental import pallas as pl
from jax.experimental.pallas import tpu as pltpu

HEAD_DIM = 64
LANES = 128
BF16_SUBLANES = 16
WINDOW = 128
BLOCK = 128
N_BUCKETS = 32
MAX_DISTANCE = 128
EPS = 1e-6
MASK_VALUE = -1e30
EXP_ZERO_BELOW = -104.0
VMEM_LIMIT_BYTES = 56 * 1024 * 1024
FUSED_MLP_VMEM_LIMIT_BYTES = 62 * 1024 * 1024
SB_GROUP = 8
SWA_GROUP = 4
SWA_CHUNK = 16

F32 = jnp.float32
BF16 = jnp.bfloat16


def _rms_scale(x):
    return lax.rsqrt(jnp.mean(x * x, axis=-1, keepdims=True) + EPS)


def _norm_matmul_kernel(x_ref, g_ref, w_ref, o_ref, *, tn):
    x = x_ref[...]
    h = (x * _rms_scale(x) * g_ref[...]).astype(BF16)
    for c0 in range(0, o_ref.shape[1], tn):
        o_ref[:, c0:c0 + tn] = jnp.dot(h, w_ref[:, c0:c0 + tn],
                                       preferred_element_type=F32).astype(o_ref.dtype)


def _norm_matmul(x, g, w, *, tm, tn):
    t, d = x.shape
    n = w.shape[1]
    return pl.pallas_call(
        functools.partial(_norm_matmul_kernel, tn=tn),
        grid=(t // tm,),
        in_specs=[
            pl.BlockSpec((tm, d), lambda i: (i, 0)),
            pl.BlockSpec((1, d), lambda i: (0, 0)),
            pl.BlockSpec((d, n), lambda i: (0, 0), pipeline_mode=pl.Buffered(1)),
        ],
        out_specs=pl.BlockSpec((tm, n), lambda i: (i, 0)),
        out_shape=jax.ShapeDtypeStruct((t, n), BF16),
        compiler_params=pltpu.CompilerParams(
            dimension_semantics=("parallel",),
            vmem_limit_bytes=VMEM_LIMIT_BYTES),
        name="norm_in_proj",
    )(x, g, w)


def _t5_bucket_np(dist):
    max_exact = N_BUCKETS // 2
    d = np.maximum(dist, 0)
    ratio = np.maximum(d, 1).astype(np.float32) / max_exact
    large = max_exact + (np.log(ratio) / math.log(MAX_DISTANCE / max_exact)
                         * (N_BUCKETS - max_exact)).astype(np.int32)
    large = np.minimum(large, N_BUCKETS - 1)
    return np.where(d < max_exact, d, large).astype(np.int32)


def _bias_table_kernel(rb_ref, bucket_ref, o_ref):
    h = pl.program_id(0)
    bucket = bucket_ref[...]
    acc = jnp.full(bucket.shape, MASK_VALUE, F32)
    for b in range(N_BUCKETS):
        acc = jnp.where(bucket == b, rb_ref[b, h], acc)
    o_ref[0] = acc


def _bias_table(rel_bias, gqa):
    n_heads = rel_bias.shape[1]
    qi = np.arange(BLOCK)[:, None]
    kj = np.arange(2 * BLOCK)[None, :]
    dist = qi + BLOCK - kj
    in_window = (dist >= 0) & (dist < WINDOW)
    bucket = np.where(in_window, _t5_bucket_np(dist), -1).astype(np.int32)
    return pl.pallas_call(
        _bias_table_kernel,
        grid=(n_heads,),
        in_specs=[
            pl.BlockSpec(memory_space=pltpu.SMEM),
            pl.BlockSpec((BLOCK, 2 * BLOCK), lambda h: (0, 0)),
        ],
        out_specs=pl.BlockSpec((1, BLOCK, 2 * BLOCK), lambda h: (h // gqa, (h % gqa) // 2, h % 2)),
        out_shape=jax.ShapeDtypeStruct((n_heads // gqa, (gqa // 2) * BLOCK, 4 * BLOCK), F32),
        name="t5_bias_table",
    )(rel_bias.astype(F32), jnp.asarray(bucket))


def _head_sumsq(xf, ones_blockdiag):
    return jnp.dot((xf * xf).astype(BF16), ones_blockdiag, preferred_element_type=F32)


def _lane_half_copies(xf, lo_half, fill):
    xr = pltpu.roll(xf, HEAD_DIM, axis=1)
    other = jnp.full_like(xf, fill)
    lo = lambda x: jnp.where(lo_half, x, other).astype(BF16)
    hi = lambda x: jnp.where(lo_half, other, x).astype(BF16)
    return [(lo(xf), hi(xr)), (lo(xr), hi(xf))]


def _swa_kernel(sink_ref, q_ref, kg_ref, kp_ref, vg_ref, vp_ref, bias_ref,
                gq_ref, gk_ref, ones_bd_ref, o_ref, *, gqa, group):
    n_inner = q_ref.shape[0] // (group * BLOCK)
    lax.fori_loop(0, n_inner, lambda s, _: _swa_group(
        s, pl.program_id(2) * n_inner + s, sink_ref, q_ref, kg_ref, kp_ref, vg_ref, vp_ref, bias_ref,
        gq_ref, gk_ref, ones_bd_ref, o_ref, gqa=gqa, group=group), 0)


def _swa_group(s, step, sink_ref, q_ref, kg_ref, kp_ref, vg_ref, vp_ref, bias_ref,
               gq_ref, gk_ref, ones_bd_ref, o_ref, *, gqa, group):
    kvp = pl.program_id(1)
    ppk = gqa // 2
    lo_half = lax.broadcasted_iota(jnp.int32, (1, LANES), 1) < HEAD_DIM
    ones_bd = ones_bd_ref[...]
    nt = (((1,), (1,)), ((), ()))
    scale = 1.0 / math.sqrt(HEAD_DIM)
    base = pl.multiple_of(s * (group * BLOCK), BLOCK)
    before = pl.multiple_of(jnp.maximum(base - BLOCK, 0), BLOCK)

    def with_previous_block(group_ref, prev_ref):
        prev = jnp.where(s == 0, prev_ref[...], group_ref[pl.ds(before, BLOCK), :])
        return jnp.concatenate([prev, group_ref[pl.ds(base, group * BLOCK), :]], axis=0).astype(F32)

    kf = with_previous_block(kg_ref, kp_ref)
    kn = kf * lax.rsqrt(_head_sumsq(kf, ones_bd) * (1.0 / HEAD_DIM) + EPS) * gk_ref[...]
    k_copies = _lane_half_copies(kn, lo_half, 0.0)
    v_copies = _lane_half_copies(with_previous_block(vg_ref, vp_ref), lo_half, 1.0)

    col = lax.broadcasted_iota(jnp.int32, (1, 2 * BLOCK), 1)
    row_top = lax.broadcasted_iota(jnp.int32, (ppk * BLOCK, 1), 0)

    for j in range(group):
        r0 = j * BLOCK
        rows = pl.ds(pl.multiple_of(base + r0, BLOCK), BLOCK)
        key_ok = jnp.logical_or(col >= BLOCK, step * group + j > 0)
        qn = []
        for r in range(2 * ppk):
            qf = q_ref[rows, r * LANES:(r + 1) * LANES].astype(F32)
            qn.append((qf * lax.rsqrt(_head_sumsq(qf, ones_bd) * (1.0 / HEAD_DIM) + EPS)
                       * (gq_ref[...] * scale)).astype(BF16))
        for c in range(2):
            k_lo, k_hi = k_copies[c]
            v_lo, v_hi = v_copies[c]
            kcat = jnp.concatenate([k_lo[r0:r0 + 2 * BLOCK], k_hi[r0:r0 + 2 * BLOCK]], axis=0)
            qs = jnp.concatenate(qn[c * ppk:(c + 1) * ppk], axis=0)
            z = lax.dot_general(qs, kcat, nt, preferred_element_type=F32)
            es, sink_terms = [], []
            for h in range(2):
                sl = slice(h * 2 * BLOCK, (h + 1) * 2 * BLOCK)
                s = jnp.where(key_ok, z[:, sl] + bias_ref[c, :, sl], MASK_VALUE)
                head0 = kvp * 2 * gqa + c * gqa + h
                sink = jnp.full((ppk * BLOCK, 1), sink_ref[head0], F32)
                for r in range(1, ppk):
                    sink = jnp.where(row_top >= r * BLOCK, sink_ref[head0 + 2 * r], sink)
                m = jnp.maximum(jnp.max(s, axis=-1, keepdims=True), sink)
                es.append(jnp.exp(s - m).astype(BF16))
                sink_terms.append(jnp.exp(sink - m))
            res = [jnp.dot(es[0], v_lo[r0:r0 + 2 * BLOCK], preferred_element_type=F32),
                   jnp.dot(es[1], v_hi[r0:r0 + 2 * BLOCK], preferred_element_type=F32)]
            num = jnp.where(lo_half, res[0], res[1])
            den = (pltpu.roll(jnp.where(lo_half, res[1], res[0]), HEAD_DIM, axis=1)
                   + jnp.where(lo_half, sink_terms[0], sink_terms[1]))
            out = (num * (1.0 / den)).astype(o_ref.dtype)
            for r in range(ppk):
                lane0 = (c * ppk + r) * LANES
                o_ref[rows, lane0:lane0 + LANES] = out[r * BLOCK:(r + 1) * BLOCK]
    return 0


def _swa_attention(proj, sinks, bias, gq2, gk2, *, batch, seq, n_heads, n_kv_heads, k_col, v_col, group, chunk):
    t = proj.shape[0]
    nb = seq // BLOCK
    ns = nb // chunk
    kv_pairs = n_kv_heads // 2
    gqa = n_heads // n_kv_heads
    qw = 2 * gqa * HEAD_DIM
    kcb = k_col // LANES
    vcb = v_col // LANES
    ones_bd = jnp.asarray(np.kron(np.eye(2), np.ones((HEAD_DIM, HEAD_DIM))), BF16)

    cur = lambda cb: (lambda b, g, i: (b * ns + i, cb + g))
    prev = lambda cb: (lambda b, g, i: (b * nb + jnp.maximum(i * chunk - 1, 0), cb + g))
    return pl.pallas_call(
        functools.partial(_swa_kernel, gqa=gqa, group=group),
        grid=(batch, kv_pairs, ns),
        in_specs=[
            pl.BlockSpec(memory_space=pltpu.SMEM),
            pl.BlockSpec((chunk * BLOCK, qw), lambda b, g, i: (b * ns + i, g)),
            pl.BlockSpec((chunk * BLOCK, LANES), cur(kcb)),
            pl.BlockSpec((BLOCK, LANES), prev(kcb)),
            pl.BlockSpec((chunk * BLOCK, LANES), cur(vcb)),
            pl.BlockSpec((BLOCK, LANES), prev(vcb)),
            pl.BlockSpec((2, (gqa // 2) * BLOCK, 4 * BLOCK), lambda b, g, i: (g, 0, 0)),
            pl.BlockSpec((1, LANES), lambda b, g, i: (0, 0)),
            pl.BlockSpec((1, LANES), lambda b, g, i: (0, 0)),
            pl.BlockSpec((LANES, LANES), lambda b, g, i: (0, 0)),
        ],
        out_specs=pl.BlockSpec((chunk * BLOCK, qw), lambda b, g, i: (b * ns + i, g)),
        out_shape=jax.ShapeDtypeStruct((t, n_heads * HEAD_DIM), BF16),
        compiler_params=pltpu.CompilerParams(
            dimension_semantics=("parallel", "parallel", "parallel"),
            vmem_limit_bytes=VMEM_LIMIT_BYTES),
        name="swa_attention",
    )(sinks, proj, proj, proj, proj, proj, bias, gq2, gk2, ones_bd)


def _sb_kernel(q_ref, k_ref, v_ref, u_ref, o_ref, acc_ref, carry_ref, z_ref, w_ref, *, tb, group):
    n_groups = q_ref.shape[0] // (group * tb)
    acc_ref[...] = jnp.zeros_like(acc_ref)
    w_ref[...] = jnp.zeros_like(w_ref)

    def body(step, d_prev):
        _sb_finish_group(jnp.maximum(step - 1, 0), d_prev, v_ref, o_ref, acc_ref, w_ref, tb=tb, group=group)
        return _sb_walk_group(step, q_ref, k_ref, v_ref, u_ref, acc_ref, carry_ref, z_ref, w_ref,
                              tb=tb, group=group)

    d_last = lax.fori_loop(0, n_groups, body, jnp.int32(1))
    _sb_finish_group(n_groups - 1, d_last, v_ref, o_ref, acc_ref, w_ref, tb=tb, group=group)


def _split_heads(x2):
    lo_half = lax.broadcasted_iota(jnp.int32, (1, LANES), 1) < HEAD_DIM
    zero = jnp.zeros_like(x2)
    return jnp.concatenate([jnp.where(lo_half, x2, zero), jnp.where(lo_half, zero, x2)], axis=0)


def _sb_finish_group(step, d_end, v_ref, o_ref, acc_ref, w_ref, *, tb, group):
    d = d_end - 1
    for i in range(group):
        start = pl.multiple_of(jnp.maximum(step * group + i - d, 0) * tb, tb)
        acc = acc_ref[i] + jnp.dot(w_ref[d & 1, i], _split_heads(v_ref[pl.ds(start, tb), :]),
                                   preferred_element_type=F32)
        o_ref[pl.ds(pl.multiple_of((step * group + i) * tb, tb), tb), :] = acc.astype(o_ref.dtype)


def _sb_walk_group(step, q_ref, k_ref, v_ref, u_ref, acc_ref, carry_ref, z_ref, w_ref, *, tb, group):
    base = step * (group * tb)
    scale = 1.0 / math.sqrt(HEAD_DIM)
    uu = u_ref[...]
    row = lax.broadcasted_iota(jnp.int32, (tb, tb), 0)
    colk = lax.broadcasted_iota(jnp.int32, (tb, tb), 1)
    below_diag = colk < row
    nt = (((1,), (1,)), ((), ()))
    split_heads = _split_heads

    acc_ref[...] = jnp.zeros_like(acc_ref)
    carry_ref[...] = jnp.zeros_like(carry_ref)

    q_heads = [split_heads((q_ref[pl.ds(pl.multiple_of(base + i * tb, tb), tb), :].astype(F32)
                            * scale).astype(BF16)) for i in range(group)]

    def tile_start(i, d):
        kb = step * group + i - d
        return pl.multiple_of(jnp.maximum(kb, 0) * tb, tb), kb >= 0

    def issue_scores(d):
        for i in range(group):
            start, _ = tile_start(i, d)
            z_ref[d & 1, i] = lax.dot_general(q_heads[i], k_ref[pl.ds(start, tb), :], nt,
                                              preferred_element_type=F32)

    def weights(d, masked):
        log_betas, xs = [], []
        for i in range(group):
            for h in range(2):
                z = z_ref[d & 1, i, h * tb:(h + 1) * tb, :]
                neg_abs = lax.bitcast_convert_type(
                    lax.bitcast_convert_type(z, jnp.int32) | jnp.int32(-2 ** 31), F32)
                soft = jnp.log(1.0 + jnp.exp(neg_abs))
                log_beta = jnp.minimum(z, 0.0) - soft
                log_1m = log_beta - z
                if masked:
                    log_1m = jnp.where(below_diag, log_1m, 0.0)
                log_betas.append(log_beta)
                xs.append(log_1m.astype(BF16))
        cs_all = jnp.dot(jnp.concatenate(xs, axis=0), uu, preferred_element_type=F32)
        worst = None
        for i in range(group):
            _, active = tile_start(i, d)
            for h in range(2):
                n = 2 * i + h
                cs = cs_all[n * tb:(n + 1) * tb]
                carry = jnp.where(active, carry_ref[i, h], MASK_VALUE)
                w = jnp.exp(log_betas[n] + cs[:, :tb] + carry)
                if masked:
                    w = jnp.where(below_diag, w, 0.0)
                w_ref[d & 1, i, :, h * tb:(h + 1) * tb] = w.astype(BF16)
                carry = carry + cs[:, tb:]
                carry_ref[i, h] = carry
                worst = carry if worst is None else jnp.maximum(worst, carry)
        return jnp.max(worst, axis=0, keepdims=True)[0, 0]

    def accumulate(d):
        for i in range(group):
            start, _ = tile_start(i, d)
            acc_ref[i] += jnp.dot(w_ref[d & 1, i], split_heads(v_ref[pl.ds(start, tb), :]),
                                  preferred_element_type=F32)

    zero = jnp.int32(0)
    issue_scores(zero)
    worst0 = weights(zero, True)
    issue_scores(zero + 1)
    last_block = step * group + group - 1

    def cond(state):
        d, worst = state
        return jnp.logical_and(d <= last_block, worst >= EXP_ZERO_BELOW)

    def body(state):
        d, _ = state
        accumulate(d - 1)
        worst = weights(d, False)
        issue_scores(d + 1)
        return d + 1, worst

    d_end, _ = lax.while_loop(cond, body, (zero + 1, worst0))
    return d_end


def _sb_attention(proj, *, batch, seq, n_heads, q_col, k_col, v_col, tb, group):
    t = proj.shape[0]
    pairs = n_heads // 2
    qcb, kcb, vcb = q_col // LANES, k_col // LANES, v_col // LANES
    tri = np.arange(tb)[:, None] > np.arange(tb)[None, :]
    uu = jnp.asarray(np.concatenate([tri, np.ones((tb, tb), bool)], axis=1), BF16)
    return pl.pallas_call(
        functools.partial(_sb_kernel, tb=tb, group=group),
        grid=(batch, pairs),
        in_specs=[
            pl.BlockSpec((seq, LANES), lambda b, p: (b, qcb + p)),
            pl.BlockSpec((seq, LANES), lambda b, p: (b, kcb + p)),
            pl.BlockSpec((seq, LANES), lambda b, p: (b, vcb + p)),
            pl.BlockSpec((tb, 2 * tb), lambda b, p: (0, 0)),
        ],
        out_specs=pl.BlockSpec((seq, LANES), lambda b, p: (b, p)),
        out_shape=jax.ShapeDtypeStruct((t, n_heads * HEAD_DIM), BF16),
        scratch_shapes=[pltpu.VMEM((group, tb, LANES), F32),
                        pltpu.VMEM((group, 2, tb, tb), F32),
                        pltpu.VMEM((2, group, 2 * tb, tb), F32),
                        pltpu.VMEM((2, group, tb, 2 * tb), BF16)],
        compiler_params=pltpu.CompilerParams(
            dimension_semantics=("parallel", "parallel"),
            vmem_limit_bytes=VMEM_LIMIT_BYTES),
        name="sb_attention",
    )(proj, proj, proj, uu)


def _mlp_kernel(oa_ref, ob_ref, ga_ref, gb_ref, wo_ref, x_ref, g_ref, wu_ref, wd_ref, *refs, n_cast):
    cast_src, o_ref, cast_dst = refs[:n_cast], refs[n_cast], refs[n_cast + 1:2 * n_cast + 1]
    h_ref, acc_ref = refs[2 * n_cast + 1:]
    f = pl.program_id(1)
    last = pl.num_programs(1) - 1

    def normed(o_r, g_r):
        o = o_r[...].astype(F32)
        return (o * _rms_scale(o) * g_r[...]).astype(BF16)

    def ffn_slice(h):
        for src, dst in zip(cast_src, cast_dst):
            dst[...] = src[...].astype(dst.dtype)
        u = jnp.maximum(jnp.dot(h, wu_ref[...], preferred_element_type=F32), 0.0)
        return jnp.dot((u * u).astype(BF16), wd_ref[...], preferred_element_type=F32)

    @pl.when(f == 0)
    def _():
        wa = oa_ref.shape[1]
        x = x_ref[...] + jnp.dot(normed(oa_ref, ga_ref), wo_ref[:wa, :], preferred_element_type=F32)
        x = x + jnp.dot(normed(ob_ref, gb_ref), wo_ref[wa:, :], preferred_element_type=F32)
        h = (x * _rms_scale(x) * g_ref[...]).astype(h_ref.dtype)
        h_ref[...] = h
        acc_ref[...] = x + ffn_slice(h)

    @pl.when(jnp.logical_and(f > 0, f < last))
    def _():
        acc_ref[...] += ffn_slice(h_ref[...])

    @pl.when(f == last)
    def _():
        o_ref[...] = acc_ref[...] + ffn_slice(h_ref[...])


def _chunking(rows, cols, n_chunks):
    for col_chunks in (1, 2, 4, 8, 16):
        row_chunks, rem = divmod(n_chunks, col_chunks)
        if rem or rows % row_chunks or cols % col_chunks:
            continue
        br, bc = rows // row_chunks, cols // col_chunks
        if br % BF16_SUBLANES == 0 and bc % LANES == 0:
            return br, bc, col_chunks
    return None


def _mlp(o_a, o_b, ga, gb, w_out, x, g, w_up, w_down, cast_weights=(), cast_layer=0, *, tm, tf):
    t, d = x.shape
    wa, wb = o_a.shape[1], o_b.shape[1]
    ff = w_up.shape[1]
    nf = ff // tf
    assert nf >= 2, "first and last d_ff slices are distinct code paths"
    n_steps = (t // tm) * nf
    cast_in, cast_out, cast_shapes = [], [], []
    for w in cast_weights:
        br, bc, cc = _chunking(w.shape[1], w.shape[2], n_steps)
        cast_in.append(pl.BlockSpec((None, br, bc), lambda i, f, cc=cc: (cast_layer, (i * nf + f) // cc, (i * nf + f) % cc)))
        cast_out.append(pl.BlockSpec((br, bc), lambda i, f, cc=cc: ((i * nf + f) // cc, (i * nf + f) % cc)))
        cast_shapes.append(jax.ShapeDtypeStruct(w.shape[1:], BF16))
    return pl.pallas_call(
        functools.partial(_mlp_kernel, n_cast=len(cast_weights)),
        grid=(t // tm, nf),
        in_specs=[
            pl.BlockSpec((tm, wa), lambda i, f: (i, 0)),
            pl.BlockSpec((tm, wb), lambda i, f: (i, 0)),
            pl.BlockSpec((1, wa), lambda i, f: (0, 0)),
            pl.BlockSpec((1, wb), lambda i, f: (0, 0)),
            pl.BlockSpec((wa + wb, d), lambda i, f: (0, 0), pipeline_mode=pl.Buffered(1)),
            pl.BlockSpec((tm, d), lambda i, f: (i, 0)),
            pl.BlockSpec((1, d), lambda i, f: (0, 0)),
            pl.BlockSpec((d, tf), lambda i, f: (0, f)),
            pl.BlockSpec((tf, d), lambda i, f: (f, 0)),
        ] + cast_in,
        out_specs=[pl.BlockSpec((tm, d), lambda i, f: (i, 0))] + cast_out,
        out_shape=[jax.ShapeDtypeStruct((t, d), F32)] + cast_shapes,
        scratch_shapes=[pltpu.VMEM((tm, d), BF16), pltpu.VMEM((tm, d), F32)],
        compiler_params=pltpu.CompilerParams(
            dimension_semantics=("parallel", "arbitrary"),
            vmem_limit_bytes=FUSED_MLP_VMEM_LIMIT_BYTES),
        name="mlp_residual",
    )(o_a, o_b, ga, gb, w_out, x, g, w_up, w_down, *cast_weights)


def _tile(total, preferred):
    if total <= preferred:
        return total
    for cand in range(preferred, 0, -LANES):
        if total % cand == 0:
            return cand
    return total


def kernel(x, norm_attn_g, w_in, q_norm_g, k_norm_g, sinks, rel_bias, swa_out_g, sb_out_g,
           w_out, norm_mlp_g, w_up, w_down):
    batch, seq, d_model = x.shape
    depth = w_in.shape[0]
    swa_heads = sinks.shape[1]
    swa_q_w = swa_heads * HEAD_DIM
    sb_w = sb_out_g.shape[1]
    sb_heads = sb_w // HEAD_DIM
    d_in = w_in.shape[2]
    swa_kv_w = (d_in - swa_q_w - 3 * sb_w) // 2
    swa_kv_heads = swa_kv_w // HEAD_DIM
    o1 = swa_q_w
    o2 = o1 + swa_kv_w
    o3 = o2 + swa_kv_w
    o4 = o3 + sb_w
    o5 = o4 + sb_w
    assert seq % BLOCK == 0 and swa_kv_heads % 2 == 0 and sb_heads % 2 == 0
    assert swa_heads % swa_kv_heads == 0 and (swa_heads // swa_kv_heads) % 2 == 0

    t = batch * seq
    tm = _tile(t, 512)
    tn_in = _tile(d_in, 1536)
    tm_mlp = tm
    tf = _tile(w_up.shape[2], 1024)
    sb_group = next(g for g in (SB_GROUP, 2, 1) if (seq // BLOCK) % g == 0)

    xt = x.reshape(t, d_model).astype(F32)
    swa_group = next(g for g in (SWA_GROUP, 1) if (seq // BLOCK) % g == 0)
    swa_chunk = next(c for c in (SWA_CHUNK, swa_group) if (seq // BLOCK) % c == 0 and c % swa_group == 0)
    bias = _bias_table(rel_bias, swa_heads // swa_kv_heads)
    row = lambda v: v.reshape(1, -1).astype(F32)

    stacked = (w_in, w_out, w_up, w_down)
    mlp_steps = (t // tm_mlp) * (w_up.shape[2] // tf)
    ride_along = all(_chunking(w.shape[1], w.shape[2], mlp_steps) is not None for w in stacked)
    layer_w = tuple(w[0].astype(BF16) for w in stacked)
    for l in range(depth):
        wl_in, wl_out, wl_up, wl_down = layer_w
        proj = _norm_matmul(xt, row(norm_attn_g[l]), wl_in, tm=tm, tn=tn_in)
        gq2 = row(jnp.tile(q_norm_g[l], LANES // HEAD_DIM))
        gk2 = row(jnp.tile(k_norm_g[l], LANES // HEAD_DIM))
        o_a = _swa_attention(proj, sinks[l].astype(F32), bias, gq2, gk2, batch=batch, seq=seq,
                             n_heads=swa_heads, n_kv_heads=swa_kv_heads, k_col=o1, v_col=o2,
                             group=swa_group, chunk=swa_chunk)
        o_b = _sb_attention(proj, batch=batch, seq=seq, n_heads=sb_heads,
                            q_col=o3, k_col=o4, v_col=o5, tb=BLOCK, group=sb_group)
        mix_args = (o_a, o_b, row(swa_out_g[l]), row(sb_out_g[l]), wl_out, xt, row(norm_mlp_g[l]), wl_up, wl_down)
        if l + 1 < depth and ride_along:
            xt, *layer_w = _mlp(*mix_args, stacked, l + 1, tm=tm_mlp, tf=tf)
        else:
            xt, = _mlp(*mix_args, tm=tm_mlp, tf=tf)
            if l + 1 < depth:
                layer_w = tuple(w[l + 1].astype(BF16) for w in stacked)
    return xt.reshape(batch, seq, d_model).astype(x.dtype)
```

```python
import functools
import math

import numpy as np
import jax
import jax.numpy as jnp
from jax import lax
from jax.experimental import pallas as pl
from jax.experimental.pallas import tpu as pltpu

HEAD_DIM = 64
LANES = 128
BF16_SUBLANES = 16
WINDOW = 128
BLOCK = 128
N_BUCKETS = 32
MAX_DISTANCE = 128
EPS = 1e-6
MASK_VALUE = -1e30
EXP_ZERO_BELOW = -104.0
VMEM_LIMIT_BYTES = 56 * 1024 * 1024
SB_GROUP = 8
SWA_GROUP = 4
SWA_CHUNK = 16

F32 = jnp.float32
BF16 = jnp.bfloat16


def _rms_scale(x):
    return lax.rsqrt(jnp.mean(x * x, axis=-1, keepdims=True) + EPS)


def _norm_matmul_kernel(x_ref, g_ref, w_ref, o_ref, *, tn):
    x = x_ref[...]
    h = (x * _rms_scale(x) * g_ref[...]).astype(BF16)
    for c0 in range(0, o_ref.shape[1], tn):
        o_ref[:, c0:c0 + tn] = jnp.dot(h, w_ref[:, c0:c0 + tn],
                                       preferred_element_type=F32).astype(o_ref.dtype)


def _norm_matmul(x, g, w, *, tm, tn):
    t, d = x.shape
    n = w.shape[1]
    return pl.pallas_call(
        functools.partial(_norm_matmul_kernel, tn=tn),
        grid=(t // tm,),
        in_specs=[
            pl.BlockSpec((tm, d), lambda i: (i, 0)),
            pl.BlockSpec((1, d), lambda i: (0, 0)),
            pl.BlockSpec((d, n), lambda i: (0, 0), pipeline_mode=pl.Buffered(1)),
        ],
        out_specs=pl.BlockSpec((tm, n), lambda i: (i, 0)),
        out_shape=jax.ShapeDtypeStruct((t, n), BF16),
        compiler_params=pltpu.CompilerParams(
            dimension_semantics=("parallel",),
            vmem_limit_bytes=VMEM_LIMIT_BYTES),
        name="norm_in_proj",
    )(x, g, w)


def _t5_bucket_np(dist):
    max_exact = N_BUCKETS // 2
    d = np.maximum(dist, 0)
    ratio = np.maximum(d, 1).astype(np.float32) / max_exact
    large = max_exact + (np.log(ratio) / math.log(MAX_DISTANCE / max_exact)
                         * (N_BUCKETS - max_exact)).astype(np.int32)
    large = np.minimum(large, N_BUCKETS - 1)
    return np.where(d < max_exact, d, large).astype(np.int32)


def _bias_table_kernel(rb_ref, bucket_ref, o_ref):
    h = pl.program_id(0)
    bucket = bucket_ref[...]
    acc = jnp.full(bucket.shape, MASK_VALUE, F32)
    for b in range(N_BUCKETS):
        acc = jnp.where(bucket == b, rb_ref[b, h], acc)
    o_ref[0] = acc


def _bias_table(rel_bias, gqa):
    n_heads = rel_bias.shape[1]
    qi = np.arange(BLOCK)[:, None]
    kj = np.arange(2 * BLOCK)[None, :]
    dist = qi + BLOCK - kj
    in_window = (dist >= 0) & (dist < WINDOW)
    bucket = np.where(in_window, _t5_bucket_np(dist), -1).astype(np.int32)
    return pl.pallas_call(
        _bias_table_kernel,
        grid=(n_heads,),
        in_specs=[
            pl.BlockSpec(memory_space=pltpu.SMEM),
            pl.BlockSpec((BLOCK, 2 * BLOCK), lambda h: (0, 0)),
        ],
        out_specs=pl.BlockSpec((1, BLOCK, 2 * BLOCK), lambda h: (h // gqa, (h % gqa) // 2, h % 2)),
        out_shape=jax.ShapeDtypeStruct((n_heads // gqa, (gqa // 2) * BLOCK, 4 * BLOCK), F32),
        name="t5_bias_table",
    )(rel_bias.astype(F32), jnp.asarray(bucket))


def _head_sumsq(xf, ones_blockdiag):
    return jnp.dot((xf * xf).astype(BF16), ones_blockdiag, preferred_element_type=F32)


def _lane_half_copies(xf, lo_half, fill):
    xr = pltpu.roll(xf, HEAD_DIM, axis=1)
    other = jnp.full_like(xf, fill)
    lo = lambda x: jnp.where(lo_half, x, other).astype(BF16)
    hi = lambda x: jnp.where(lo_half, other, x).astype(BF16)
    return [(lo(xf), hi(xr)), (lo(xr), hi(xf))]


def _swa_kernel(sink_ref, q_ref, kg_ref, kp_ref, vg_ref, vp_ref, bias_ref,
                gq_ref, gk_ref, ones_bd_ref, o_ref, *, gqa, group):
    n_inner = q_ref.shape[0] // (group * BLOCK)
    lax.fori_loop(0, n_inner, lambda s, _: _swa_group(
        s, pl.program_id(2) * n_inner + s, sink_ref, q_ref, kg_ref, kp_ref, vg_ref, vp_ref, bias_ref,
        gq_ref, gk_ref, ones_bd_ref, o_ref, gqa=gqa, group=group), 0)


def _swa_group(s, step, sink_ref, q_ref, kg_ref, kp_ref, vg_ref, vp_ref, bias_ref,
               gq_ref, gk_ref, ones_bd_ref, o_ref, *, gqa, group):
    kvp = pl.program_id(1)
    ppk = gqa // 2
    lo_half = lax.broadcasted_iota(jnp.int32, (1, LANES), 1) < HEAD_DIM
    ones_bd = ones_bd_ref[...]
    nt = (((1,), (1,)), ((), ()))
    scale = 1.0 / math.sqrt(HEAD_DIM)
    base = pl.multiple_of(s * (group * BLOCK), BLOCK)
    before = pl.multiple_of(jnp.maximum(base - BLOCK, 0), BLOCK)

    def with_previous_block(group_ref, prev_ref):
        prev = jnp.where(s == 0, prev_ref[...], group_ref[pl.ds(before, BLOCK), :])
        return jnp.concatenate([prev, group_ref[pl.ds(base, group * BLOCK), :]], axis=0).astype(F32)

    kf = with_previous_block(kg_ref, kp_ref)
    kn = kf * lax.rsqrt(_head_sumsq(kf, ones_bd) * (1.0 / HEAD_DIM) + EPS) * gk_ref[...]
    k_copies = _lane_half_copies(kn, lo_half, 0.0)
    v_copies = _lane_half_copies(with_previous_block(vg_ref, vp_ref), lo_half, 1.0)

    col = lax.broadcasted_iota(jnp.int32, (1, 2 * BLOCK), 1)
    row_top = lax.broadcasted_iota(jnp.int32, (ppk * BLOCK, 1), 0)

    for j in range(group):
        r0 = j * BLOCK
        rows = pl.ds(pl.multiple_of(base + r0, BLOCK), BLOCK)
        key_ok = jnp.logical_or(col >= BLOCK, step * group + j > 0)
        qf = jnp.concatenate([q_ref[rows, r * LANES:(r + 1) * LANES] for r in range(2 * ppk)],
                             axis=0).astype(F32)
        qall = (qf * lax.rsqrt(_head_sumsq(qf, ones_bd) * (1.0 / HEAD_DIM) + EPS)
                * (gq_ref[...] * scale)).astype(BF16)
        qn = [qall[r * BLOCK:(r + 1) * BLOCK] for r in range(2 * ppk)]
        for c in range(2):
            k_lo, k_hi = k_copies[c]
            v_lo, v_hi = v_copies[c]
            kcat = jnp.concatenate([k_lo[r0:r0 + 2 * BLOCK], k_hi[r0:r0 + 2 * BLOCK]], axis=0)
            qs = jnp.concatenate(qn[c * ppk:(c + 1) * ppk], axis=0)
            z = lax.dot_general(qs, kcat, nt, preferred_element_type=F32)
            es, sink_terms = [], []
            for h in range(2):
                sl = slice(h * 2 * BLOCK, (h + 1) * 2 * BLOCK)
                s = jnp.where(key_ok, z[:, sl] + bias_ref[c, :, sl], MASK_VALUE)
                head0 = kvp * 2 * gqa + c * gqa + h
                sink = jnp.full((ppk * BLOCK, 1), sink_ref[head0], F32)
                for r in range(1, ppk):
                    sink = jnp.where(row_top >= r * BLOCK, sink_ref[head0 + 2 * r], sink)
                m = jnp.maximum(jnp.max(s, axis=-1, keepdims=True), sink)
                es.append(jnp.exp(s - m).astype(BF16))
                sink_terms.append(jnp.exp(sink - m))
            res = [jnp.dot(es[0], v_lo[r0:r0 + 2 * BLOCK], preferred_element_type=F32),
                   jnp.dot(es[1], v_hi[r0:r0 + 2 * BLOCK], preferred_element_type=F32)]
            num = jnp.where(lo_half, res[0], res[1])
            den = (pltpu.roll(jnp.where(lo_half, res[1], res[0]), HEAD_DIM, axis=1)
                   + jnp.where(lo_half, sink_terms[0], sink_terms[1]))
            out = (num * (1.0 / den)).astype(o_ref.dtype)
            for r in range(ppk):
                lane0 = (c * ppk + r) * LANES
                o_ref[rows, lane0:lane0 + LANES] = out[r * BLOCK:(r + 1) * BLOCK]
    return 0


def _swa_attention(proj, sinks, bias, gq2, gk2, *, batch, seq, n_heads, n_kv_heads, k_col, v_col, group, chunk):
    t = proj.shape[0]
    nb = seq // BLOCK
    ns = nb // chunk
    kv_pairs = n_kv_heads // 2
    gqa = n_heads // n_kv_heads
    qw = 2 * gqa * HEAD_DIM
    kcb = k_col // LANES
    vcb = v_col // LANES
    ones_bd = jnp.asarray(np.kron(np.eye(2), np.ones((HEAD_DIM, HEAD_DIM))), BF16)

    cur = lambda cb: (lambda b, g, i: (b * ns + i, cb + g))
    prev = lambda cb: (lambda b, g, i: (b * nb + jnp.maximum(i * chunk - 1, 0), cb + g))
    return pl.pallas_call(
        functools.partial(_swa_kernel, gqa=gqa, group=group),
        grid=(batch, kv_pairs, ns),
        in_specs=[
            pl.BlockSpec(memory_space=pltpu.SMEM),
            pl.BlockSpec((chunk * BLOCK, qw), lambda b, g, i: (b * ns + i, g)),
            pl.BlockSpec((chunk * BLOCK, LANES), cur(kcb)),
            pl.BlockSpec((BLOCK, LANES), prev(kcb)),
            pl.BlockSpec((chunk * BLOCK, LANES), cur(vcb)),
            pl.BlockSpec((BLOCK, LANES), prev(vcb)),
            pl.BlockSpec((2, (gqa // 2) * BLOCK, 4 * BLOCK), lambda b, g, i: (g, 0, 0)),
            pl.BlockSpec((1, LANES), lambda b, g, i: (0, 0)),
            pl.BlockSpec((1, LANES), lambda b, g, i: (0, 0)),
            pl.BlockSpec((LANES, LANES), lambda b, g, i: (0, 0)),
        ],
        out_specs=pl.BlockSpec((chunk * BLOCK, qw), lambda b, g, i: (b * ns + i, g)),
        out_shape=jax.ShapeDtypeStruct((t, n_heads * HEAD_DIM), BF16),
        compiler_params=pltpu.CompilerParams(
            dimension_semantics=("parallel", "parallel", "parallel"),
            vmem_limit_bytes=VMEM_LIMIT_BYTES),
        name="swa_attention",
    )(sinks, proj, proj, proj, proj, proj, bias, gq2, gk2, ones_bd)


def _sb_kernel(q_ref, k_ref, v_ref, u_ref, o_ref, acc_ref, carry_ref, z_ref, w_ref, *, tb, group):
    n_groups = q_ref.shape[0] // (group * tb)
    acc_ref[...] = jnp.zeros_like(acc_ref)
    w_ref[...] = jnp.zeros_like(w_ref)

    def body(step, d_prev):
        _sb_finish_group(jnp.maximum(step - 1, 0), d_prev, v_ref, o_ref, acc_ref, w_ref, tb=tb, group=group)
        return _sb_walk_group(step, q_ref, k_ref, v_ref, u_ref, acc_ref, carry_ref, z_ref, w_ref,
                              tb=tb, group=group)

    d_last = lax.fori_loop(0, n_groups, body, jnp.int32(1))
    _sb_finish_group(n_groups - 1, d_last, v_ref, o_ref, acc_ref, w_ref, tb=tb, group=group)


def _split_heads(x2):
    lo_half = lax.broadcasted_iota(jnp.int32, (1, LANES), 1) < HEAD_DIM
    zero = jnp.zeros_like(x2)
    return jnp.concatenate([jnp.where(lo_half, x2, zero), jnp.where(lo_half, zero, x2)], axis=0)


def _sb_finish_group(step, d_end, v_ref, o_ref, acc_ref, w_ref, *, tb, group):
    d = d_end - 1
    for i in range(group):
        start = pl.multiple_of(jnp.maximum(step * group + i - d, 0) * tb, tb)
        acc = acc_ref[i] + jnp.dot(w_ref[d & 1, i], _split_heads(v_ref[pl.ds(start, tb), :]),
                                   preferred_element_type=F32)
        o_ref[pl.ds(pl.multiple_of((step * group + i) * tb, tb), tb), :] = acc.astype(o_ref.dtype)


def _sb_walk_group(step, q_ref, k_ref, v_ref, u_ref, acc_ref, carry_ref, z_ref, w_ref, *, tb, group):
    base = step * (group * tb)
    scale = 1.0 / math.sqrt(HEAD_DIM)
    uu = u_ref[...]
    row = lax.broadcasted_iota(jnp.int32, (tb, tb), 0)
    colk = lax.broadcasted_iota(jnp.int32, (tb, tb), 1)
    below_diag = colk < row
    nt = (((1,), (1,)), ((), ()))
    split_heads = _split_heads

    acc_ref[...] = jnp.zeros_like(acc_ref)
    carry_ref[...] = jnp.zeros_like(carry_ref)

    q_heads = [split_heads((q_ref[pl.ds(pl.multiple_of(base + i * tb, tb), tb), :].astype(F32)
                            * scale).astype(BF16)) for i in range(group)]

    def tile_start(i, d):
        kb = step * group + i - d
        return pl.multiple_of(jnp.maximum(kb, 0) * tb, tb), kb >= 0

    def issue_scores(d):
        for i in range(group):
            start, _ = tile_start(i, d)
            z_ref[d & 1, i] = lax.dot_general(q_heads[i], k_ref[pl.ds(start, tb), :], nt,
                                              preferred_element_type=F32)

    def weights(d, masked):
        log_betas, xs = [], []
        for i in range(group):
            for h in range(2):
                z = z_ref[d & 1, i, h * tb:(h + 1) * tb, :]
                neg_abs = lax.bitcast_convert_type(
                    lax.bitcast_convert_type(z, jnp.int32) | jnp.int32(-2 ** 31), F32)
                soft = jnp.log(1.0 + jnp.exp(neg_abs))
                log_beta = jnp.minimum(z, 0.0) - soft
                log_1m = log_beta - z
                if masked:
                    log_1m = jnp.where(below_diag, log_1m, 0.0)
                log_betas.append(log_beta)
                xs.append(log_1m.astype(BF16))
        cs_all = jnp.dot(jnp.concatenate(xs, axis=0), uu, preferred_element_type=F32)
        worst = None
        for i in range(group):
            _, active = tile_start(i, d)
            for h in range(2):
                n = 2 * i + h
                cs = cs_all[n * tb:(n + 1) * tb]
                carry = jnp.where(active, carry_ref[i, h], MASK_VALUE)
                w = jnp.exp(log_betas[n] + cs[:, :tb] + carry)
                if masked:
                    w = jnp.where(below_diag, w, 0.0)
                w_ref[d & 1, i, :, h * tb:(h + 1) * tb] = w.astype(BF16)
                carry = carry + cs[:, tb:]
                carry_ref[i, h] = carry
                worst = carry if worst is None else jnp.maximum(worst, carry)
        return jnp.max(worst, axis=0, keepdims=True)[0, 0]

    def accumulate(d):
        for i in range(group):
            start, _ = tile_start(i, d)
            acc_ref[i] += jnp.dot(w_ref[d & 1, i], split_heads(v_ref[pl.ds(start, tb), :]),
                                  preferred_element_type=F32)

    zero = jnp.int32(0)
    issue_scores(zero)
    worst0 = weights(zero, True)
    issue_scores(zero + 1)
    last_block = step * group + group - 1

    def cond(state):
        d, worst = state
        return jnp.logical_and(d <= last_block, worst >= EXP_ZERO_BELOW)

    def body(state):
        d, _ = state
        accumulate(d - 1)
        worst = weights(d, False)
        issue_scores(d + 1)
        return d + 1, worst

    d_end, _ = lax.while_loop(cond, body, (zero + 1, worst0))
    return d_end


def _sb_attention(proj, *, batch, seq, n_heads, q_col, k_col, v_col, tb, group):
    t = proj.shape[0]
    pairs = n_heads // 2
    qcb, kcb, vcb = q_col // LANES, k_col // LANES, v_col // LANES
    tri = np.arange(tb)[:, None] > np.arange(tb)[None, :]
    uu = jnp.asarray(np.concatenate([tri, np.ones((tb, tb), bool)], axis=1), BF16)
    return pl.pallas_call(
        functools.partial(_sb_kernel, tb=tb, group=group),
        grid=(batch, pairs),
        in_specs=[
            pl.BlockSpec((seq, LANES), lambda b, p: (b, qcb + p)),
            pl.BlockSpec((seq, LANES), lambda b, p: (b, kcb + p)),
            pl.BlockSpec((seq, LANES), lambda b, p: (b, vcb + p)),
            pl.BlockSpec((tb, 2 * tb), lambda b, p: (0, 0)),
        ],
        out_specs=pl.BlockSpec((seq, LANES), lambda b, p: (b, p)),
        out_shape=jax.ShapeDtypeStruct((t, n_heads * HEAD_DIM), BF16),
        scratch_shapes=[pltpu.VMEM((group, tb, LANES), F32),
                        pltpu.VMEM((group, 2, tb, tb), F32),
                        pltpu.VMEM((2, group, 2 * tb, tb), F32),
                        pltpu.VMEM((2, group, tb, 2 * tb), BF16)],
        compiler_params=pltpu.CompilerParams(
            dimension_semantics=("parallel", "parallel"),
            vmem_limit_bytes=VMEM_LIMIT_BYTES),
        name="sb_attention",
    )(proj, proj, proj, uu)


def _out_proj_kernel(oa_ref, ob_ref, ga_ref, gb_ref, w_ref, x_ref, o_ref):
    def normed(o_r, g_r):
        o = o_r[...].astype(F32)
        return (o * _rms_scale(o) * g_r[...]).astype(BF16)

    wa = oa_ref.shape[1]
    acc = jnp.dot(normed(oa_ref, ga_ref), w_ref[:wa, :], preferred_element_type=F32)
    acc += jnp.dot(normed(ob_ref, gb_ref), w_ref[wa:, :], preferred_element_type=F32)
    o_ref[...] = x_ref[...] + acc


def _out_proj(o_a, o_b, ga, gb, w, x, *, tm):
    t, d = x.shape
    wa, wb = o_a.shape[1], o_b.shape[1]
    return pl.pallas_call(
        _out_proj_kernel,
        grid=(t // tm,),
        in_specs=[
            pl.BlockSpec((tm, wa), lambda i: (i, 0)),
            pl.BlockSpec((tm, wb), lambda i: (i, 0)),
            pl.BlockSpec((1, wa), lambda i: (0, 0)),
            pl.BlockSpec((1, wb), lambda i: (0, 0)),
            pl.BlockSpec((wa + wb, d), lambda i: (0, 0), pipeline_mode=pl.Buffered(1)),
            pl.BlockSpec((tm, d), lambda i: (i, 0)),
        ],
        out_specs=pl.BlockSpec((tm, d), lambda i: (i, 0)),
        out_shape=jax.ShapeDtypeStruct((t, d), F32),
        compiler_params=pltpu.CompilerParams(
            dimension_semantics=("parallel",),
            vmem_limit_bytes=VMEM_LIMIT_BYTES),
        name="out_proj_residual",
    )(o_a, o_b, ga, gb, w, x)


def _mlp_kernel(x_ref, g_ref, wu_ref, wd_ref, *refs, n_cast):
    cast_src, o_ref, cast_dst = refs[:n_cast], refs[n_cast], refs[n_cast + 1:2 * n_cast + 1]
    h_ref, acc_ref = refs[2 * n_cast + 1:]
    f = pl.program_id(1)
    last = pl.num_programs(1) - 1

    def ffn_slice(h):
        for src, dst in zip(cast_src, cast_dst):
            dst[...] = src[...].astype(dst.dtype)
        u = jnp.maximum(jnp.dot(h, wu_ref[...], preferred_element_type=F32), 0.0)
        return jnp.dot((u * u).astype(BF16), wd_ref[...], preferred_element_type=F32)

    @pl.when(f == 0)
    def _():
        x = x_ref[...]
        h = (x * _rms_scale(x) * g_ref[...]).astype(h_ref.dtype)
        h_ref[...] = h
        acc_ref[...] = ffn_slice(h)

    @pl.when(jnp.logical_and(f > 0, f < last))
    def _():
        acc_ref[...] += ffn_slice(h_ref[...])

    @pl.when(f == last)
    def _():
        o_ref[...] = x_ref[...] + acc_ref[...] + ffn_slice(h_ref[...])


def _chunking(rows, cols, n_chunks):
    for col_chunks in (1, 2, 4, 8, 16):
        row_chunks, rem = divmod(n_chunks, col_chunks)
        if rem or rows % row_chunks or cols % col_chunks:
            continue
        br, bc = rows // row_chunks, cols // col_chunks
        if br % BF16_SUBLANES == 0 and bc % LANES == 0:
            return br, bc, col_chunks
    return None


def _mlp(x, g, w_up, w_down, cast_weights=(), cast_layer=0, *, tm, tf):
    t, d = x.shape
    ff = w_up.shape[1]
    nf = ff // tf
    assert nf >= 2, "first and last d_ff slices are distinct code paths"
    n_steps = (t // tm) * nf
    cast_in, cast_out, cast_shapes = [], [], []
    for w in cast_weights:
        br, bc, cc = _chunking(w.shape[1], w.shape[2], n_steps)
        cast_in.append(pl.BlockSpec((None, br, bc), lambda i, f, cc=cc: (cast_layer, (i * nf + f) // cc, (i * nf + f) % cc)))
        cast_out.append(pl.BlockSpec((br, bc), lambda i, f, cc=cc: ((i * nf + f) // cc, (i * nf + f) % cc)))
        cast_shapes.append(jax.ShapeDtypeStruct(w.shape[1:], BF16))
    return pl.pallas_call(
        functools.partial(_mlp_kernel, n_cast=len(cast_weights)),
        grid=(t // tm, nf),
        in_specs=[
            pl.BlockSpec((tm, d), lambda i, f: (i, 0)),
            pl.BlockSpec((1, d), lambda i, f: (0, 0)),
            pl.BlockSpec((d, tf), lambda i, f: (0, f)),
            pl.BlockSpec((tf, d), lambda i, f: (f, 0)),
        ] + cast_in,
        out_specs=[pl.BlockSpec((tm, d), lambda i, f: (i, 0))] + cast_out,
        out_shape=[jax.ShapeDtypeStruct((t, d), F32)] + cast_shapes,
        scratch_shapes=[pltpu.VMEM((tm, d), BF16), pltpu.VMEM((tm, d), F32)],
        compiler_params=pltpu.CompilerParams(
            dimension_semantics=("parallel", "arbitrary"),
            vmem_limit_bytes=VMEM_LIMIT_BYTES),
        name="mlp_residual",
    )(x, g, w_up, w_down, *cast_weights)


def _tile(total, preferred):
    if total <= preferred:
        return total
    for cand in range(preferred, 0, -LANES):
        if total % cand == 0:
            return cand
    return total


def kernel(x, norm_attn_g, w_in, q_norm_g, k_norm_g, sinks, rel_bias, swa_out_g, sb_out_g,
           w_out, norm_mlp_g, w_up, w_down):
    batch, seq, d_model = x.shape
    depth = w_in.shape[0]
    swa_heads = sinks.shape[1]
    swa_q_w = swa_heads * HEAD_DIM
    sb_w = sb_out_g.shape[1]
    sb_heads = sb_w // HEAD_DIM
    d_in = w_in.shape[2]
    swa_kv_w = (d_in - swa_q_w - 3 * sb_w) // 2
    swa_kv_heads = swa_kv_w // HEAD_DIM
    o1 = swa_q_w
    o2 = o1 + swa_kv_w
    o3 = o2 + swa_kv_w
    o4 = o3 + sb_w
    o5 = o4 + sb_w
    assert seq % BLOCK == 0 and swa_kv_heads % 2 == 0 and sb_heads % 2 == 0
    assert swa_heads % swa_kv_heads == 0 and (swa_heads // swa_kv_heads) % 2 == 0

    t = batch * seq
    tm = _tile(t, 512)
    tn_in = _tile(d_in, 1536)
    tm_mlp = tm
    tf = _tile(w_up.shape[2], 1024)
    sb_group = next(g for g in (SB_GROUP, 2, 1) if (seq // BLOCK) % g == 0)

    xt = x.reshape(t, d_model).astype(F32)
    swa_group = next(g for g in (SWA_GROUP, 1) if (seq // BLOCK) % g == 0)
    swa_chunk = next(c for c in (SWA_CHUNK, swa_group) if (seq // BLOCK) % c == 0 and c % swa_group == 0)
    bias = _bias_table(rel_bias, swa_heads // swa_kv_heads)
    row = lambda v: v.reshape(1, -1).astype(F32)

    stacked = (w_in, w_out, w_up, w_down)
    mlp_steps = (t // tm_mlp) * (w_up.shape[2] // tf)
    ride_along = all(_chunking(w.shape[1], w.shape[2], mlp_steps) is not None for w in stacked)
    layer_w = tuple(w[0].astype(BF16) for w in stacked)
    for l in range(depth):
        wl_in, wl_out, wl_up, wl_down = layer_w
        proj = _norm_matmul(xt, row(norm_attn_g[l]), wl_in, tm=tm, tn=tn_in)
        gq2 = row(jnp.tile(q_norm_g[l], LANES // HEAD_DIM))
        gk2 = row(jnp.tile(k_norm_g[l], LANES // HEAD_DIM))
        o_a = _swa_attention(proj, sinks[l].astype(F32), bias, gq2, gk2, batch=batch, seq=seq,
                             n_heads=swa_heads, n_kv_heads=swa_kv_heads, k_col=o1, v_col=o2,
                             group=swa_group, chunk=swa_chunk)
        o_b = _sb_attention(proj, batch=batch, seq=seq, n_heads=sb_heads,
                            q_col=o3, k_col=o4, v_col=o5, tb=BLOCK, group=sb_group)
        xt = _out_proj(o_a, o_b, row(swa_out_g[l]), row(sb_out_g[l]), wl_out, xt, tm=tm)
        if l + 1 < depth and ride_along:
            xt, *layer_w = _mlp(xt, row(norm_mlp_g[l]), wl_up, wl_down, stacked, l + 1, tm=tm_mlp, tf=tf)
        else:
            xt, = _mlp(xt, row(norm_mlp_g[l]), wl_up, wl_down, tm=tm_mlp, tf=tf)
            if l + 1 < depth:
                layer_w = tuple(w[l + 1].astype(BF16) for w in stacked)
    return xt.reshape(batch, seq, d_model).astype(x.dtype)
```
